```python
import jax, jax.numpy as jnp
from jax import lax
import numpy as np

D_MODEL = 2048
BATCH = 4
SEQ = 8192
DEPTH = 4
DEC_BATCH = 16
DEC_SEQ = 64
PAST_LEN = 2048

CHUNK = 64
N_MIXERS = 2
N_SGU_LAYERS = (DEPTH + 1) // 2
N_MLA_LAYERS = DEPTH // 2
SGU_CHUNK = 128
D_SGU = D_MODEL
SGU_GROUPS = 8
SGU_GROUP_DIM = D_SGU // SGU_GROUPS
MLA_HEADS = 16
Q_LORA = 512
KV_LORA = 512
QK_NOPE_DIM = 128
QK_ROPE_DIM = 64
V_HEAD_DIM = 128
ROPE_THETA = 10000.0
Q_BLOCK = 128
MEM_TOKENS = 256
XA_HEADS = 4
XA_HEAD_DIM = 128
D_FF = 7168
N_EXPERTS = 8
TOP_K = 2
D_EXPERT = 7168
MOE_BLOCK = 256
NORM_EPS = 1e-6
NEG_INF = -1e30

kernel_name = 'streaming_sgu_mla_hybrid_step'


def rms_norm(x, g):
    xf = x.astype(jnp.float32)
    y = xf * lax.rsqrt(jnp.mean(xf * xf, axis=-1, keepdims=True) + NORM_EPS)
    return (y * g.astype(jnp.float32)).astype(x.dtype)


def layer_norm(x, g, b):
    xf = x.astype(jnp.float32)
    mu = jnp.mean(xf, axis=-1, keepdims=True)
    xc = xf - mu
    var = jnp.mean(xc * xc, axis=-1, keepdims=True)
    return (xc * lax.rsqrt(var + NORM_EPS) * g.astype(jnp.float32) + b.astype(jnp.float32)).astype(x.dtype)


def apply_rope(x, pos):
    half = QK_ROPE_DIM // 2
    inv_freq = ROPE_THETA ** (-jnp.arange(half, dtype=jnp.float32) / half)
    ang = pos.astype(jnp.float32)[:, None] * inv_freq[None, :]
    cos = jnp.cos(ang)[None, :, None, :]
    sin = jnp.sin(ang)[None, :, None, :]
    xf = x.astype(jnp.float32)
    x1, x2 = xf[..., :half], xf[..., half:]
    return jnp.concatenate([x1 * cos - x2 * sin, x1 * sin + x2 * cos], axis=-1).astype(x.dtype)


def sgu_mixer(h, w_in, b_in, ln_g, ln_b, w_s, b_s, w_out):
    B, S, _ = h.shape
    z = jax.nn.gelu(h @ w_in + b_in, approximate=False)
    u, v = z[..., :D_SGU], z[..., D_SGU:]
    v = layer_norm(v, ln_g, ln_b)
    L = min(S, SGU_CHUNK)
    nc = S // L
    causal = jnp.tril(jnp.ones((L, L), dtype=bool))
    w = jnp.where(causal[None], w_s[:, :L, :L], 0.0).astype(v.dtype)
    vg = v.reshape(B, nc, L, SGU_GROUPS, SGU_GROUP_DIM)
    bias = b_s[:, :L].T[None, None, :, :, None].astype(v.dtype)
    mixed = jnp.einsum('gts,bcsgd->bctgd', w, vg) + bias
    y = (u * mixed.reshape(B, S, D_SGU)) @ w_out
    return y, v


def mla_project(h, pos, w_dq, g_q, w_uq, w_dkv, g_kv):
    B, S, _ = h.shape
    cq = rms_norm(h @ w_dq, g_q)
    q = (cq @ w_uq).reshape(B, S, MLA_HEADS, QK_NOPE_DIM + QK_ROPE_DIM)
    q_nope = q[..., :QK_NOPE_DIM]
    q_pe = apply_rope(q[..., QK_NOPE_DIM:], pos)
    kv = h @ w_dkv
    ckv = rms_norm(kv[..., :KV_LORA], g_kv)
    k_pe = apply_rope(kv[..., KV_LORA:][:, :, None, :], pos)[:, :, 0, :]
    return q_nope, q_pe, ckv, k_pe


def chunk_causal_attention(q_nope, q_pe, k_nope, k_pe, v, q_pos, k_pos):
    B, Sq, H, _ = q_nope.shape
    scale = (QK_NOPE_DIM + QK_ROPE_DIM) ** -0.5
    k_chunk = k_pos // CHUNK

    def attend(args):
        qn, qr, qpos = args
        s = (jnp.einsum('bqhd,bkhd->bhqk', qn, k_nope)
             + jnp.einsum('bqhr,bkr->bhqk', qr, k_pe)).astype(jnp.float32) * scale
        mask = k_chunk[None, :] <= (qpos // CHUNK)[:, None]
        s = jnp.where(mask[None, None], s, NEG_INF)
        p = jax.nn.softmax(s, axis=-1).astype(v.dtype)
        return jnp.einsum('bhqk,bkhd->bqhd', p, v)

    if Sq > Q_BLOCK and Sq % Q_BLOCK == 0:
        nb = Sq // Q_BLOCK
        blk = lambda t: jnp.moveaxis(t.reshape((B, nb, Q_BLOCK) + t.shape[2:]), 1, 0)
        out = lax.map(attend, (blk(q_nope), blk(q_pe), q_pos.reshape(nb, Q_BLOCK)))
        return jnp.moveaxis(out, 0, 1).reshape(B, Sq, H, V_HEAD_DIM)
    return attend((q_nope, q_pe, q_pos))


def mla_attend(q_nope, q_pe, ckv, k_pe, q_pos, k_pos, w_uk, w_uv, w_o):
    B, Sq = q_nope.shape[:2]
    k_nope = jnp.einsum('bkc,chd->bkhd', ckv, w_uk.reshape(KV_LORA, MLA_HEADS, QK_NOPE_DIM))
    v = jnp.einsum('bkc,chd->bkhd', ckv, w_uv.reshape(KV_LORA, MLA_HEADS, V_HEAD_DIM))
    o = chunk_causal_attention(q_nope, q_pe, k_nope, k_pe, v, q_pos, k_pos)
    return o.reshape(B, Sq, MLA_HEADS * V_HEAD_DIM) @ w_o


def memory_kv(mem, g_mem, w_k, w_v):
    B, M, _ = mem.shape
    m = rms_norm(mem, g_mem)
    k = (m @ w_k).reshape(B, M, XA_HEADS, XA_HEAD_DIM)
    v = (m @ w_v).reshape(B, M, XA_HEADS, XA_HEAD_DIM)
    return k, v


def memory_cross_attention(h, mem_k, mem_v, w_q, w_o):
    B, S, _ = h.shape
    q = (h @ w_q).reshape(B, S, XA_HEADS, XA_HEAD_DIM)
    s = jnp.einsum('bqhd,bmhd->bhqm', q, mem_k).astype(jnp.float32) * (XA_HEAD_DIM ** -0.5)
    p = jax.nn.softmax(s, axis=-1).astype(mem_v.dtype)
    o = jnp.einsum('bhqm,bmhd->bqhd', p, mem_v)
    return o.reshape(B, S, XA_HEADS * XA_HEAD_DIM) @ w_o


def swiglu(h, w_gate, w_up, w_down):
    return (jax.nn.silu(h @ w_gate) * (h @ w_up)) @ w_down


def moe_swiglu(h, w_router, w_gate, w_up, w_down):
    B, S, D = h.shape
    x2 = h.reshape(B * S, D)
    N = B * S
    logits = (x2 @ w_router).astype(jnp.float32)
    top_logit, top_idx = lax.top_k(logits, TOP_K)
    gates = jax.nn.softmax(top_logit, axis=-1)
    A = N * TOP_K
    flat_e = top_idx.reshape(A)
    flat_tok = jnp.repeat(jnp.arange(N, dtype=jnp.int32), TOP_K)
    flat_g = gates.reshape(A)
    order = jnp.argsort(flat_e)
    e_sorted = flat_e[order]
    counts = jnp.bincount(flat_e, length=N_EXPERTS)
    starts = jnp.cumsum(counts) - counts
    padded = (counts + MOE_BLOCK - 1) // MOE_BLOCK * MOE_BLOCK
    pad_ends = jnp.cumsum(padded)
    pad_starts = pad_ends - padded
    dest = pad_starts[e_sorted] + (jnp.arange(A, dtype=jnp.int32) - starts[e_sorted])
    n_blocks = -(-(A + N_EXPERTS * (MOE_BLOCK - 1)) // MOE_BLOCK)
    P = n_blocks * MOE_BLOCK
    slot_tok = jnp.full((P,), N, dtype=jnp.int32).at[dest].set(flat_tok[order])
    slot_gate = jnp.zeros((P,), jnp.float32).at[dest].set(flat_g[order])
    block_start = jnp.arange(n_blocks, dtype=jnp.int32) * MOE_BLOCK
    block_expert = jnp.minimum(jnp.sum(pad_ends[None, :] <= block_start[:, None], axis=1), N_EXPERTS - 1)
    x_pad = jnp.concatenate([x2, jnp.zeros((1, D), x2.dtype)], axis=0)
    xb = x_pad[slot_tok].reshape(n_blocks, MOE_BLOCK, D)

    def expert_block(args):
        xblk, e = args
        hid = jax.nn.silu(xblk @ w_gate[e]) * (xblk @ w_up[e])
        return hid @ w_down[e]

    yb = lax.map(expert_block, (xb, block_expert)).reshape(P, D)
    y = jnp.zeros((N + 1, D), h.dtype).at[slot_tok].add(yb * slot_gate[:, None].astype(yb.dtype))
    return y[:N].reshape(B, S, D)


def setup_inputs(seed: int = 0) -> dict:
    key = jax.random.key(seed)
    ks = iter(jax.random.split(key, 40))
    f32 = jnp.float32

    def nrm(shape, scale):
        return jax.random.normal(next(ks), shape, f32) * scale

    def gain(shape):
        return 1.0 + nrm(shape, 0.01)

    H_QK = MLA_HEADS * (QK_NOPE_DIM + QK_ROPE_DIM)
    XA_W = XA_HEADS * XA_HEAD_DIM
    return {
        'x_prompt': nrm((BATCH, SEQ, D_MODEL), 1.0),
        'x_sample': nrm((DEC_BATCH, DEC_SEQ, D_MODEL), 1.0),
        'cache_l1_ckv': nrm((DEC_BATCH, PAST_LEN, KV_LORA), 1.0),
        'cache_l1_kpe': nrm((DEC_BATCH, PAST_LEN, QK_ROPE_DIM), 1.0),
        'cache_l3_ckv': nrm((DEC_BATCH, PAST_LEN, KV_LORA), 1.0),
        'cache_l3_kpe': nrm((DEC_BATCH, PAST_LEN, QK_ROPE_DIM), 1.0),
        'cache_mem_k': nrm((DEPTH, DEC_BATCH, MEM_TOKENS, XA_HEADS, XA_HEAD_DIM), 1.0),
        'cache_mem_v': nrm((DEPTH, DEC_BATCH, MEM_TOKENS, XA_HEADS, XA_HEAD_DIM), 1.0),
        'mem_prompt': nrm((BATCH, MEM_TOKENS, D_MODEL), 1.0),
        'g_mix': gain((DEPTH, D_MODEL)),
        'g_xattn': gain((DEPTH, D_MODEL)),
        'g_mem': gain((DEPTH, D_MODEL)),
        'g_ffn': gain((DEPTH, D_MODEL)),
        'g_final': gain((D_MODEL,)),
        'sgu_w_in': nrm((N_SGU_LAYERS, D_MODEL, 2 * D_SGU), D_MODEL ** -0.5),
        'sgu_b_in': nrm((N_SGU_LAYERS, 2 * D_SGU), 0.01),
        'sgu_ln_g': gain((N_SGU_LAYERS, D_SGU)),
        'sgu_ln_b': nrm((N_SGU_LAYERS, D_SGU), 0.01),
        'sgu_w_s': nrm((N_SGU_LAYERS, SGU_GROUPS, SGU_CHUNK, SGU_CHUNK), SGU_CHUNK ** -0.5),
        'sgu_b_s': gain((N_SGU_LAYERS, SGU_GROUPS, SGU_CHUNK)),
        'sgu_w_out': nrm((N_SGU_LAYERS, D_SGU, D_MODEL), D_SGU ** -0.5),
        'mla_w_dq': nrm((N_MLA_LAYERS, D_MODEL, Q_LORA), D_MODEL ** -0.5),
        'mla_g_q': gain((N_MLA_LAYERS, Q_LORA)),
        'mla_w_uq': nrm((N_MLA_LAYERS, Q_LORA, H_QK), Q_LORA ** -0.5),
        'mla_w_dkv': nrm((N_MLA_LAYERS, D_MODEL, KV_LORA + QK_ROPE_DIM), D_MODEL ** -0.5),
        'mla_g_kv': gain((N_MLA_LAYERS, KV_LORA)),
        'mla_w_uk': nrm((N_MLA_LAYERS, KV_LORA, MLA_HEADS * QK_NOPE_DIM), KV_LORA ** -0.5),
        'mla_w_uv': nrm((N_MLA_LAYERS, KV_LORA, MLA_HEADS * V_HEAD_DIM), KV_LORA ** -0.5),
        'mla_w_o': nrm((N_MLA_LAYERS, MLA_HEADS * V_HEAD_DIM, D_MODEL), (MLA_HEADS * V_HEAD_DIM) ** -0.5),
        'xa_w_q': nrm((DEPTH, D_MODEL, XA_W), D_MODEL ** -0.5),
        'xa_w_k': nrm((DEPTH, D_MODEL, XA_W), D_MODEL ** -0.5),
        'xa_w_v': nrm((DEPTH, D_MODEL, XA_W), D_MODEL ** -0.5),
        'xa_w_o': nrm((DEPTH, XA_W, D_MODEL), XA_W ** -0.5),
        'ffn_w_gate': nrm((N_SGU_LAYERS, D_MODEL, D_FF), D_MODEL ** -0.5),
        'ffn_w_up': nrm((N_SGU_LAYERS, D_MODEL, D_FF), D_MODEL ** -0.5),
        'ffn_w_down': nrm((N_SGU_LAYERS, D_FF, D_MODEL), D_FF ** -0.5),
        'moe_w_router': nrm((N_MLA_LAYERS, D_MODEL, N_EXPERTS), D_MODEL ** -0.5),
        'moe_w_gate': nrm((N_MLA_LAYERS, N_EXPERTS, D_MODEL, D_EXPERT), D_MODEL ** -0.5),
        'moe_w_up': nrm((N_MLA_LAYERS, N_EXPERTS, D_MODEL, D_EXPERT), D_MODEL ** -0.5),
        'moe_w_down': nrm((N_MLA_LAYERS, N_EXPERTS, D_EXPERT, D_MODEL), D_EXPERT ** -0.5),
    }


def reference(x_prompt, x_sample, cache_l1_ckv, cache_l1_kpe, cache_l3_ckv, cache_l3_kpe,
              cache_mem_k, cache_mem_v, mem_prompt,
              g_mix, g_xattn, g_mem, g_ffn, g_final,
              sgu_w_in, sgu_b_in, sgu_ln_g, sgu_ln_b, sgu_w_s, sgu_b_s, sgu_w_out,
              mla_w_dq, mla_g_q, mla_w_uq, mla_w_dkv, mla_g_kv, mla_w_uk, mla_w_uv, mla_w_o,
              xa_w_q, xa_w_k, xa_w_v, xa_w_o,
              ffn_w_gate, ffn_w_up, ffn_w_down,
              moe_w_router, moe_w_gate, moe_w_up, moe_w_down):
    S = x_prompt.shape[1]
    Sd = x_sample.shape[1]
    P_len = cache_l1_ckv.shape[1]
    pos_p = jnp.arange(S, dtype=jnp.int32)
    pos_s = P_len + jnp.arange(Sd, dtype=jnp.int32)
    pos_all = jnp.arange(P_len + Sd, dtype=jnp.int32)
    mla_cache = [(cache_l1_ckv, cache_l1_kpe), (cache_l3_ckv, cache_l3_kpe)]

    xp, xs = x_prompt, x_sample
    ckv_p, kpe_p, ckv_s, kpe_s, sgu_v_s, mem_k_p, mem_v_p = [], [], [], [], [], [], []
    for layer in range(DEPTH):
        i = layer // N_MIXERS
        hp = rms_norm(xp, g_mix[layer])
        hs = rms_norm(xs, g_mix[layer])
        if layer % N_MIXERS == 0:
            sgu_args = (sgu_w_in[i], sgu_b_in[i], sgu_ln_g[i], sgu_ln_b[i], sgu_w_s[i], sgu_b_s[i], sgu_w_out[i])
            yp, _ = sgu_mixer(hp, *sgu_args)
            ys, v_new = sgu_mixer(hs, *sgu_args)
            sgu_v_s.append(v_new)
        else:
            proj = (mla_w_dq[i], mla_g_q[i], mla_w_uq[i], mla_w_dkv[i], mla_g_kv[i])
            outw = (mla_w_uk[i], mla_w_uv[i], mla_w_o[i])
            qn, qr, ckv, kpe = mla_project(hp, pos_p, *proj)
            yp = mla_attend(qn, qr, ckv, kpe, pos_p, pos_p, *outw)
            ckv_p.append(ckv)
            kpe_p.append(kpe)
            qn_s, qr_s, ckv_new, kpe_new = mla_project(hs, pos_s, *proj)
            c_ckv, c_kpe = mla_cache[i]
            ys = mla_attend(qn_s, qr_s, jnp.concatenate([c_ckv, ckv_new], axis=1),
                            jnp.concatenate([c_kpe, kpe_new], axis=1), pos_s, pos_all, *outw)
            ckv_s.append(ckv_new)
            kpe_s.append(kpe_new)
        xp = xp + yp
        xs = xs + ys
        mk, mv = memory_kv(mem_prompt, g_mem[layer], xa_w_k[layer], xa_w_v[layer])
        mem_k_p.append(mk)
        mem_v_p.append(mv)
        xp = xp + memory_cross_attention(rms_norm(xp, g_xattn[layer]), mk, mv, xa_w_q[layer], xa_w_o[layer])
        xs = xs + memory_cross_attention(rms_norm(xs, g_xattn[layer]), cache_mem_k[layer], cache_mem_v[layer],
                                         xa_w_q[layer], xa_w_o[layer])
        hp = rms_norm(xp, g_ffn[layer])
        hs = rms_norm(xs, g_ffn[layer])
        if layer % 2 == 0:
            xp = xp + swiglu(hp, ffn_w_gate[i], ffn_w_up[i], ffn_w_down[i])
            xs = xs + swiglu(hs, ffn_w_gate[i], ffn_w_up[i], ffn_w_down[i])
        else:
            xp = xp + moe_swiglu(hp, moe_w_router[i], moe_w_gate[i], moe_w_up[i], moe_w_down[i])
            xs = xs + moe_swiglu(hs, moe_w_router[i], moe_w_gate[i], moe_w_up[i], moe_w_down[i])

    y_prompt = rms_norm(xp, g_final)
    y_sample = rms_norm(xs, g_final)
    new_mem_k_prompt = jnp.stack(mem_k_p, axis=0)
    new_mem_v_prompt = jnp.stack(mem_v_p, axis=0)
    return (y_prompt, y_sample,
            ckv_p[0], kpe_p[0], ckv_p[1], kpe_p[1],
            new_mem_k_prompt, new_mem_v_prompt,
            ckv_s[0], kpe_s[0], ckv_s[1], kpe_s[1],
            sgu_v_s[0], sgu_v_s[1])
```

```python
from functools import partial

import numpy as np
import jax
import jax.numpy as jnp
from jax import lax
from jax.experimental import pallas as pl
from jax.experimental.pallas import tpu as pltpu

F32 = jnp.float32
BF16 = jnp.bfloat16

CHUNK = 64
SGU_CHUNK = 128
SGU_GROUPS = 8
MLA_HEADS = 16
QK_NOPE_DIM = 128
QK_ROPE_DIM = 64
V_HEAD_DIM = 128
KV_LORA = 512
ROPE_THETA = 10000.0
XA_HEADS = 4
XA_HEAD_DIM = 128
N_EXPERTS = 8
TOP_K = 2
NORM_EPS = 1e-6

LANES = 128
Q_HEAD_W = 2 * LANES
VMEM_LIMIT = 56 * 1024 * 1024
NEG_BIG = -1e30
ROW_TILE = 512
MIX_TILE = 256
FF_TILE = 512


def _cparams(sem):
    return pltpu.CompilerParams(dimension_semantics=sem, vmem_limit_bytes=VMEM_LIMIT)


def _rms(x, g):
    return x * lax.rsqrt(jnp.mean(x * x, axis=-1, keepdims=True) + NORM_EPS) * g


def _dot(a, b):
    return jnp.dot(a, b, preferred_element_type=F32)


def _dot_nt(a, b):
    return lax.dot_general(a, b, (((1,), (1,)), ((), ())), preferred_element_type=F32)


def _col_chunks(n, width):
    return [(c, min(width, n - c)) for c in range(0, n, width)]


def _sgu_in_kernel(x_ref, g_ref, w_ref, b_ref, lng_ref, lnb_ref, z_ref, *rest, emit_v):
    if emit_v:
        v_ref, zs_ref = rest
    else:
        (zs_ref,) = rest
    j = pl.program_id(0)
    h = _rms(x_ref[...], g_ref[...]).astype(BF16)
    for c, w in _col_chunks(w_ref.shape[1], 512):
        a = _dot(h, w_ref[:, c:c + w]) + b_ref[:, c:c + w]
        zs_ref[:, c:c + w] = 0.5 * a * (1.0 + lax.erf(a * np.float32(np.sqrt(0.5))))

    @pl.when(j == 0)
    def _():
        z_ref[...] = zs_ref[...].astype(z_ref.dtype)

    @pl.when(j == 1)
    def _():
        z = zs_ref[...]
        mu = jnp.mean(z, axis=-1, keepdims=True)
        zc = z - mu
        var = jnp.mean(zc * zc, axis=-1, keepdims=True)
        v = zc * lax.rsqrt(var + NORM_EPS) * lng_ref[...] + lnb_ref[...]
        z_ref[...] = v.astype(z_ref.dtype)
        if emit_v:
            v_ref[...] = v


def sgu_in(x, g, w_in, b_in, ln_g, ln_b, *, tm, emit_v):
    n, d = x.shape
    ds = w_in.shape[1] // 2
    nt = n // tm
    out_shape = [jax.ShapeDtypeStruct((n, 2 * ds), BF16)]
    out_specs = [pl.BlockSpec((tm, ds), lambda j, i: (i, j))]
    if emit_v:
        out_shape.append(jax.ShapeDtypeStruct((n, ds), F32))
        out_specs.append(pl.BlockSpec((tm, ds), lambda j, i: (i * j, 0)))
    res = pl.pallas_call(
        partial(_sgu_in_kernel, emit_v=emit_v),
        grid=(2, nt),
        in_specs=[
            pl.BlockSpec((tm, d), lambda j, i: (i, 0)),
            pl.BlockSpec((1, d), lambda j, i: (0, 0)),
            pl.BlockSpec((d, ds), lambda j, i: (0, j)),
            pl.BlockSpec((1, ds), lambda j, i: (0, j)),
            pl.BlockSpec((1, ds), lambda j, i: (0, 0)),
            pl.BlockSpec((1, ds), lambda j, i: (0, 0)),
        ],
        out_specs=out_specs,
        out_shape=out_shape,
        scratch_shapes=[pltpu.VMEM((tm, ds), F32)],
        compiler_params=_cparams(("arbitrary", "arbitrary")),
        name="sgu_in",
    )(x, g.reshape(1, d), w_in, b_in.reshape(1, -1), ln_g.reshape(1, ds), ln_b.reshape(1, ds))
    return res if emit_v else (res[0], None)


def _sgu_mix_kernel(x_ref, u_ref, v_ref, ws_ref, bs_ref, wo_ref, o_ref, gs_ref, *, seg):
    tm = x_ref.shape[0]
    gd = v_ref.shape[1] // SGU_GROUPS
    row = lax.broadcasted_iota(jnp.int32, (SGU_CHUNK, SGU_CHUNK), 0)
    col = lax.broadcasted_iota(jnp.int32, (SGU_CHUNK, SGU_CHUNK), 1)
    keep = (col <= row) & ((col // seg) == (row // seg))
    for g in range(SGU_GROUPS):
        wg = jnp.where(keep, ws_ref[g], 0.0).astype(BF16)
        bg = bs_ref[g]
        for c in range(tm // SGU_CHUNK):
            rs = slice(c * SGU_CHUNK, (c + 1) * SGU_CHUNK)
            cs = slice(g * gd, (g + 1) * gd)
            mixed = _dot(wg, v_ref[rs, cs]) + bg
            gs_ref[rs, cs] = (u_ref[rs, cs].astype(F32) * mixed).astype(BF16)
    for c, w in _col_chunks(wo_ref.shape[1], 512):
        o_ref[:, c:c + w] = x_ref[:, c:c + w] + _dot(gs_ref[...], wo_ref[:, c:c + w])


def sgu_mix(x, z, w_s, b_s, w_out, *, tm, seg):
    n, d = x.shape
    ds = z.shape[1] // 2
    return pl.pallas_call(
        partial(_sgu_mix_kernel, seg=seg),
        grid=(n // tm,),
        in_specs=[
            pl.BlockSpec((tm, d), lambda i: (i, 0)),
            pl.BlockSpec((tm, ds), lambda i: (i, 0)),
            pl.BlockSpec((tm, ds), lambda i: (i, 1)),
            pl.BlockSpec(w_s.shape, lambda i: (0, 0, 0)),
            pl.BlockSpec(b_s.shape, lambda i: (0, 0, 0)),
            pl.BlockSpec(w_out.shape, lambda i: (0, 0)),
        ],
        out_specs=pl.BlockSpec((tm, d), lambda i: (i, 0)),
        out_shape=jax.ShapeDtypeStruct((n, d), F32),
        scratch_shapes=[pltpu.VMEM((tm, ds), BF16)],
        compiler_params=_cparams(("parallel",)),
        name="sgu_mix",
    )(x, z, z, w_s, b_s, w_out)


def _mem_kv_kernel(m_ref, g_ref, wk_ref, wv_ref, k_ref, v_ref):
    h = _rms(m_ref[...], g_ref[0]).astype(BF16)
    k_ref[0] = _dot(h, wk_ref[0])
    v_ref[0] = _dot(h, wv_ref[0])


def mem_kv(mem, g_mem, w_k, w_v, *, tm):
    r, d = mem.shape
    nl, _, w = w_k.shape
    return pl.pallas_call(
        _mem_kv_kernel,
        grid=(nl, r // tm),
        in_specs=[
            pl.BlockSpec((tm, d), lambda l, i: (i, 0)),
            pl.BlockSpec((1, 1, d), lambda l, i: (l, 0, 0)),
            pl.BlockSpec((1, d, w), lambda l, i: (l, 0, 0)),
            pl.BlockSpec((1, d, w), lambda l, i: (l, 0, 0)),
        ],
        out_specs=[pl.BlockSpec((1, tm, w), lambda l, i: (l, i, 0))] * 2,
        out_shape=[jax.ShapeDtypeStruct((nl, r, w), F32)] * 2,
        compiler_params=_cparams(("parallel", "parallel")),
        name="mem_kv",
    )(mem, g_mem.reshape(nl, 1, d), w_k, w_v)


def _xattn_kernel(x_ref, g_ref, wq_ref, k_ref, v_ref, wo_ref, o_ref, os_ref):
    x = x_ref[0]
    h = _rms(x, g_ref[...]).astype(BF16)
    q = (_dot(h, wq_ref[...]) * np.float32(XA_HEAD_DIM ** -0.5)).astype(BF16)
    for hd in range(XA_HEADS):
        cs = slice(hd * XA_HEAD_DIM, (hd + 1) * XA_HEAD_DIM)
        s = _dot_nt(q[:, cs], k_ref[0, :, cs])
        p = jnp.exp(s - jnp.max(s, axis=-1, keepdims=True))
        l = jnp.sum(p, axis=-1, keepdims=True)
        os_ref[:, cs] = (_dot(p.astype(BF16), v_ref[0, :, cs]) / l).astype(BF16)
    for c, w in _col_chunks(wo_ref.shape[1], 512):
        o_ref[0, :, c:c + w] = x[:, c:c + w] + _dot(os_ref[...], wo_ref[:, c:c + w])


def xattn(x, g, w_q, mem_k, mem_v, w_o, *, tm):
    b, s, d = x.shape
    m, xw = mem_k.shape[1:]
    return pl.pallas_call(
        _xattn_kernel,
        grid=(b, s // tm),
        in_specs=[
            pl.BlockSpec((1, tm, d), lambda bi, i: (bi, i, 0)),
            pl.BlockSpec((1, d), lambda bi, i: (0, 0)),
            pl.BlockSpec((d, xw), lambda bi, i: (0, 0)),
            pl.BlockSpec((1, m, xw), lambda bi, i: (bi, 0, 0)),
            pl.BlockSpec((1, m, xw), lambda bi, i: (bi, 0, 0)),
            pl.BlockSpec((xw, d), lambda bi, i: (0, 0)),
        ],
        out_specs=pl.BlockSpec((1, tm, d), lambda bi, i: (bi, i, 0)),
        out_shape=jax.ShapeDtypeStruct((b, s, d), F32),
        scratch_shapes=[pltpu.VMEM((tm, xw), BF16)],
        compiler_params=_cparams(("parallel", "parallel")),
        name="xattn",
    )(x, g.reshape(1, d), w_q, mem_k, mem_v, w_o)


def _glu_kernel(be_ref, nb_ref, x_ref, g_ref, wg_ref, wu_ref, wd_ref, o_ref, h_ref, *, dense):
    i = pl.program_id(0)
    f = pl.program_id(1)

    @pl.when(i < nb_ref[0])
    def _():
        @pl.when(f == 0)
        def _():
            if dense:
                x = x_ref[...]
                h_ref[...] = _rms(x, g_ref[...]).astype(BF16)
                o_ref[...] = x
            else:
                h_ref[...] = x_ref[...].astype(BF16)
                o_ref[...] = jnp.zeros_like(o_ref)

        h = h_ref[...]
        a = _dot(h, wg_ref[0])
        b = _dot(h, wu_ref[0])
        hid = (a * jax.nn.sigmoid(a) * b).astype(BF16)
        o_ref[...] += _dot(hid, wd_ref[0])

    @pl.when(jnp.logical_and(i >= nb_ref[0], f == 0))
    def _():
        o_ref[...] = jnp.zeros_like(o_ref)


def glu(x, g, w_gate, w_up, w_down, block_expert, n_blocks_used, *, tm, tf, dense):
    r, d = x.shape
    ff = w_gate.shape[2]
    nf = ff // tf

    def wsel(i, f, be, nb):
        live = i < nb[0]
        return be[i], jnp.where(live, f, nf - 1)

    grid_spec = pltpu.PrefetchScalarGridSpec(
        num_scalar_prefetch=2,
        grid=(r // tm, nf),
        in_specs=[
            pl.BlockSpec((tm, d), lambda i, f, be, nb: (i, 0)),
            pl.BlockSpec((1, d), lambda i, f, be, nb: (0, 0)),
            pl.BlockSpec((1, d, tf), lambda i, f, be, nb: (wsel(i, f, be, nb)[0], 0, wsel(i, f, be, nb)[1])),
            pl.BlockSpec((1, d, tf), lambda i, f, be, nb: (wsel(i, f, be, nb)[0], 0, wsel(i, f, be, nb)[1])),
            pl.BlockSpec((1, tf, d), lambda i, f, be, nb: (wsel(i, f, be, nb)[0], wsel(i, f, be, nb)[1], 0)),
        ],
        out_specs=pl.BlockSpec((tm, d), lambda i, f, be, nb: (i, 0)),
        scratch_shapes=[pltpu.VMEM((tm, d), BF16)],
    )
    return pl.pallas_call(
        partial(_glu_kernel, dense=dense),
        grid_spec=grid_spec,
        out_shape=jax.ShapeDtypeStruct((r, d), F32),
        compiler_params=_cparams(("arbitrary", "arbitrary")),
        name="glu_dense" if dense else "glu_expert",
    )(block_expert, n_blocks_used, x, g.reshape(1, d), w_gate, w_up, w_down)


def _mla_proj_kernel(x_ref, g_ref, wdq_ref, gq_ref, wuq_ref, wrot_ref, wkv_ref, gkv_ref,
                     ce_ref, co_ref, se_ref, so_ref, ck_ref, sk_ref,
                     q_ref, ckv_ref, ckvb_ref, kpe_ref, cq_ref):
    h = _rms(x_ref[...], g_ref[...]).astype(BF16)
    cq_ref[...] = _rms(_dot(h, wdq_ref[...]), gq_ref[...]).astype(BF16)
    kv = _dot(h, wkv_ref[...])
    ckv = _rms(kv[:, :KV_LORA], gkv_ref[...])
    ckv_ref[...] = ckv
    ckvb_ref[...] = ckv.astype(BF16)
    kpe_ref[...] = (kv[:, KV_LORA:KV_LORA + LANES] * ck_ref[...]
                    + kv[:, KV_LORA + LANES:KV_LORA + 2 * LANES] * sk_ref[...])
    scale = np.float32((QK_NOPE_DIM + QK_ROPE_DIM) ** -0.5)
    cq = cq_ref[...]
    for p in range(MLA_HEADS // 2):
        a = _dot(cq, wuq_ref[:, 2 * p * Q_HEAD_W:(2 * p + 2) * Q_HEAD_W])
        r = _dot(cq, wrot_ref[:, p * LANES:(p + 1) * LANES])
        for k, (c_ref, s_ref) in enumerate(((ce_ref, se_ref), (co_ref, so_ref))):
            base = (2 * p + k) * Q_HEAD_W
            q_ref[:, base:base + LANES] = (a[:, k * Q_HEAD_W:k * Q_HEAD_W + LANES] * scale).astype(BF16)
            pe = a[:, k * Q_HEAD_W + LANES:(k + 1) * Q_HEAD_W] * c_ref[...] + r * s_ref[...]
            q_ref[:, base + LANES:base + Q_HEAD_W] = (pe * scale).astype(BF16)


def mla_proj(x, g, w_dq, g_q, w_uq_ext, w_uq_rot, w_dkv_ext, g_kv, tabs, *, tm, tab_blocks):
    n, d = x.shape
    ql = w_dq.shape[1]
    full = lambda a: pl.BlockSpec(a.shape, lambda i: (0,) * a.ndim)
    tab_spec = pl.BlockSpec((tm, LANES), lambda i: (i % tab_blocks, 0))
    qw = MLA_HEADS * Q_HEAD_W
    return pl.pallas_call(
        _mla_proj_kernel,
        grid=(n // tm,),
        in_specs=[pl.BlockSpec((tm, d), lambda i: (i, 0)), pl.BlockSpec((1, d), lambda i: (0, 0)),
                  full(w_dq), pl.BlockSpec((1, ql), lambda i: (0, 0)), full(w_uq_ext), full(w_uq_rot),
                  full(w_dkv_ext), pl.BlockSpec((1, KV_LORA), lambda i: (0, 0))] + [tab_spec] * 6,
        out_specs=[pl.BlockSpec((tm, qw), lambda i: (i, 0)),
                   pl.BlockSpec((tm, KV_LORA), lambda i: (i, 0)),
                   pl.BlockSpec((tm, KV_LORA), lambda i: (i, 0)),
                   pl.BlockSpec((tm, LANES), lambda i: (i, 0))],
        out_shape=[jax.ShapeDtypeStruct((n, qw), BF16),
                   jax.ShapeDtypeStruct((n, KV_LORA), F32),
                   jax.ShapeDtypeStruct((n, KV_LORA), BF16),
                   jax.ShapeDtypeStruct((n, LANES), F32)],
        scratch_shapes=[pltpu.VMEM((tm, ql), BF16)],
        compiler_params=_cparams(("parallel",)),
        name="mla_proj",
    )(x, g.reshape(1, d), w_dq, g_q.reshape(1, ql), w_uq_ext, w_uq_rot, w_dkv_ext,
      g_kv.reshape(1, KV_LORA), *tabs)


def _kv_up_kernel(c_ref, wk_ref, wv_ref, k_ref, v_ref):
    c = c_ref[...]
    for s, w in _col_chunks(wk_ref.shape[1], 512):
        k_ref[:, s:s + w] = _dot(c, wk_ref[:, s:s + w]).astype(BF16)
        v_ref[:, s:s + w] = _dot(c, wv_ref[:, s:s + w]).astype(BF16)


def kv_up(ckv, w_uk, w_uv, *, tm):
    n, c = ckv.shape
    w = w_uk.shape[1]
    return pl.pallas_call(
        _kv_up_kernel,
        grid=(n // tm,),
        in_specs=[pl.BlockSpec((tm, c), lambda i: (i, 0)),
                  pl.BlockSpec((c, w), lambda i: (0, 0)),
                  pl.BlockSpec((c, w), lambda i: (0, 0))],
        out_specs=[pl.BlockSpec((tm, w), lambda i: (i, 0))] * 2,
        out_shape=[jax.ShapeDtypeStruct((n, w), BF16)] * 2,
        compiler_params=_cparams(("parallel",)),
        name="kv_up",
    )(ckv, w_uk, w_uv)


def _attn_kernel(q_ref, kn_ref, kp_ref, v_ref, o_ref, m_ref, l_ref, acc_ref, *, causal, tq, tk, kv_len):
    qi = pl.program_id(2)
    ki = pl.program_id(3)
    nk = pl.num_programs(3)

    @pl.when(ki == 0)
    def _():
        m_ref[...] = jnp.full_like(m_ref, NEG_BIG)
        l_ref[...] = jnp.zeros_like(l_ref)
        acc_ref[...] = jnp.zeros_like(acc_ref)

    def step(masked):
        k = jnp.concatenate([kn_ref[0], kp_ref[0]], axis=1)
        s = _dot_nt(q_ref[0], k)
        if masked:
            qc = (qi * tq + lax.broadcasted_iota(jnp.int32, (tq, tk), 0)) // CHUNK
            kc = (ki * tk + lax.broadcasted_iota(jnp.int32, (tq, tk), 1)) // CHUNK
            s = jnp.where(kc <= qc, s, NEG_BIG)
        elif kv_len is not None:
            kpos = ki * tk + lax.broadcasted_iota(jnp.int32, (tq, tk), 1)
            s = jnp.where(kpos < kv_len, s, NEG_BIG)
        m_old = m_ref[...]
        m_new = jnp.maximum(m_old, jnp.max(s, axis=-1, keepdims=True))
        alpha = jnp.exp(m_old - m_new)
        p = jnp.exp(s - m_new)
        l_ref[...] = alpha * l_ref[...] + jnp.sum(p, axis=-1, keepdims=True)
        acc_ref[...] = alpha * acc_ref[...] + _dot(p.astype(BF16), v_ref[0])
        m_ref[...] = m_new

    if causal:
        pl.when(ki < qi)(lambda: step(False))
        pl.when(ki == qi)(lambda: step(True))
    else:
        step(False)

    @pl.when(ki == nk - 1)
    def _():
        o_ref[0] = (acc_ref[...] / l_ref[...]).astype(o_ref.dtype)


def attention(q, kn, kp, v, *, tq, tk, causal, kv_len=None):
    b, sq, _ = q.shape
    sk = kn.shape[1]
    nq, nk = sq // tq, sk // tk
    if causal:
        assert tq == tk and sq == sk and kv_len is None
        kidx = lambda qi, ki: jnp.minimum(ki, qi)
    else:
        kidx = lambda qi, ki: ki
    return pl.pallas_call(
        partial(_attn_kernel, causal=causal, tq=tq, tk=tk, kv_len=kv_len),
        grid=(b, MLA_HEADS, nq, nk),
        in_specs=[
            pl.BlockSpec((1, tq, Q_HEAD_W), lambda bi, h, qi, ki: (bi, qi, h)),
            pl.BlockSpec((1, tk, QK_NOPE_DIM), lambda bi, h, qi, ki: (bi, kidx(qi, ki), h)),
            pl.BlockSpec((1, tk, LANES), lambda bi, h, qi, ki: (bi, kidx(qi, ki), 0)),
            pl.BlockSpec((1, tk, V_HEAD_DIM), lambda bi, h, qi, ki: (bi, kidx(qi, ki), h)),
        ],
        out_specs=pl.BlockSpec((1, tq, V_HEAD_DIM), lambda bi, h, qi, ki: (bi, qi, h)),
        out_shape=jax.ShapeDtypeStruct((b, sq, MLA_HEADS * V_HEAD_DIM), BF16),
        scratch_shapes=[pltpu.VMEM((tq, 1), F32), pltpu.VMEM((tq, 1), F32),
                        pltpu.VMEM((tq, V_HEAD_DIM), F32)],
        compiler_params=_cparams(("parallel", "parallel", "parallel", "arbitrary")),
        name="attention",
    )(q, kn, kp, v)


def _proj_res_kernel(x_ref, a_ref, w_ref, o_ref):
    for c, w in _col_chunks(w_ref.shape[1], 512):
        o_ref[:, c:c + w] = x_ref[:, c:c + w] + _dot(a_ref[...], w_ref[:, c:c + w])


def proj_res(x, a, w, *, tm):
    n, d = x.shape
    k = a.shape[1]
    return pl.pallas_call(
        _proj_res_kernel,
        grid=(n // tm,),
        in_specs=[pl.BlockSpec((tm, d), lambda i: (i, 0)),
                  pl.BlockSpec((tm, k), lambda i: (i, 0)),
                  pl.BlockSpec((k, d), lambda i: (0, 0))],
        out_specs=pl.BlockSpec((tm, d), lambda i: (i, 0)),
        out_shape=jax.ShapeDtypeStruct((n, d), F32),
        compiler_params=_cparams(("parallel",)),
        name="proj_res",
    )(x, a, w)


def _route_kernel(x_ref, g_ref, wr_ref, h_ref, idx_ref, gate_ref):
    h = _rms(x_ref[...], g_ref[...])
    h_ref[...] = h
    logits = jnp.dot(h, wr_ref[...], preferred_element_type=F32, precision=lax.Precision.HIGHEST)
    lane_i = lax.broadcasted_iota(jnp.int32, logits.shape, 1)
    lane = lane_i.astype(F32)
    logits = jnp.where(lane_i < N_EXPERTS, logits, -jnp.inf)
    m1 = jnp.max(logits, axis=-1, keepdims=True)
    i1 = jnp.min(jnp.where(logits == m1, lane, float(LANES)), axis=-1, keepdims=True)
    rest = jnp.where(lane == i1, -jnp.inf, logits)
    m2 = jnp.max(rest, axis=-1, keepdims=True)
    i2 = jnp.min(jnp.where(rest == m2, lane, float(LANES)), axis=-1, keepdims=True)
    e2 = jnp.exp(m2 - m1)
    g1 = 1.0 / (1.0 + e2)
    g2 = e2 / (1.0 + e2)
    idx_ref[...] = jnp.where(lane_i == 0, i1, jnp.where(lane_i == 1, i2, 0.0)).astype(jnp.int32)
    gate_ref[...] = jnp.where(lane_i == 0, g1, jnp.where(lane_i == 1, g2, 0.0))


def route(x, g, w_router_pad, *, tm):
    n, d = x.shape
    return pl.pallas_call(
        _route_kernel,
        grid=(n // tm,),
        in_specs=[pl.BlockSpec((tm, d), lambda i: (i, 0)),
                  pl.BlockSpec((1, d), lambda i: (0, 0)),
                  pl.BlockSpec((d, LANES), lambda i: (0, 0))],
        out_specs=[pl.BlockSpec((tm, d), lambda i: (i, 0)),
                   pl.BlockSpec((tm, LANES), lambda i: (i, 0)),
                   pl.BlockSpec((tm, LANES), lambda i: (i, 0))],
        out_shape=[jax.ShapeDtypeStruct((n, d), F32),
                   jax.ShapeDtypeStruct((n, LANES), jnp.int32),
                   jax.ShapeDtypeStruct((n, LANES), F32)],
        compiler_params=_cparams(("parallel",)),
        name="moe_route",
    )(x, g.reshape(1, d), w_router_pad)


def _row_copy(src_ref, src_row, dst_ref, dst_row, sem):
    return pltpu.make_async_copy(src_ref.at[pl.ds(src_row, 1)], dst_ref.at[pl.ds(dst_row, 1)], sem)


def _dispatch_kernel(dest_ref, h_ref, zero_ref, xs_ref, sem, *, tm):
    del zero_ref
    base = pl.program_id(0) * tm

    def issue(t, c):
        for k in range(TOP_K):
            _row_copy(h_ref, base + t, xs_ref, dest_ref[TOP_K * (base + t) + k], sem).start()
        return c

    lax.fori_loop(0, tm, issue, 0, unroll=8)

    def drain(t, c):
        for k in range(TOP_K):
            _row_copy(h_ref, 0, xs_ref, 0, sem).wait()
        return c

    lax.fori_loop(0, tm, drain, 0, unroll=8)


def dispatch(dest, h, n_slots, *, tm):
    n, d = h.shape
    zeros = jnp.zeros((n_slots, d), h.dtype)
    grid_spec = pltpu.PrefetchScalarGridSpec(
        num_scalar_prefetch=1,
        grid=(n // tm,),
        in_specs=[pl.BlockSpec(memory_space=pl.ANY), pl.BlockSpec(memory_space=pl.ANY)],
        out_specs=pl.BlockSpec(memory_space=pl.ANY),
        scratch_shapes=[pltpu.SemaphoreType.DMA],
    )
    return pl.pallas_call(
        partial(_dispatch_kernel, tm=tm),
        grid_spec=grid_spec,
        out_shape=jax.ShapeDtypeStruct((n_slots, d), h.dtype),
        input_output_aliases={2: 0},
        compiler_params=_cparams(("arbitrary",)),
        name="moe_dispatch",
    )(dest, h, zeros)


def _combine_kernel(dest_ref, x_ref, gate_ref, yb_ref, o_ref, y0_ref, y1_ref, sem, *, tm):
    base = pl.program_id(0) * tm
    bufs = (y0_ref, y1_ref)

    def issue(t, c):
        for k in range(TOP_K):
            _row_copy(yb_ref, dest_ref[TOP_K * (base + t) + k], bufs[k], t, sem).start()
        return c

    lax.fori_loop(0, tm, issue, 0, unroll=8)

    def drain(t, c):
        for k in range(TOP_K):
            _row_copy(yb_ref, 0, bufs[k], 0, sem).wait()
        return c

    lax.fori_loop(0, tm, drain, 0, unroll=8)
    g0 = gate_ref[:, 0:1]
    g1 = gate_ref[:, 1:2]
    o_ref[...] = x_ref[...] + (g0 * y0_ref[...] + g1 * y1_ref[...])


def combine(dest, x, gates, yb, *, tm):
    n, d = x.shape
    grid_spec = pltpu.PrefetchScalarGridSpec(
        num_scalar_prefetch=1,
        grid=(n // tm,),
        in_specs=[pl.BlockSpec((tm, d), lambda i, ds: (i, 0)),
                  pl.BlockSpec((tm, LANES), lambda i, ds: (i, 0)),
                  pl.BlockSpec(memory_space=pl.ANY)],
        out_specs=pl.BlockSpec((tm, d), lambda i, ds: (i, 0)),
        scratch_shapes=[pltpu.VMEM((tm, d), F32), pltpu.VMEM((tm, d), F32), pltpu.SemaphoreType.DMA],
    )
    return pl.pallas_call(
        partial(_combine_kernel, tm=tm),
        grid_spec=grid_spec,
        out_shape=jax.ShapeDtypeStruct((n, d), F32),
        compiler_params=_cparams(("arbitrary",)),
        name="moe_combine",
    )(dest, x, gates, yb)


def moe(x, g, w_router_pad, w_gate, w_up, w_down, *, tm_tok, tm_blk, tf):
    n, d = x.shape
    h, idx, gates = route(x, g, w_router_pad, tm=tm_tok)
    flat_e = idx[:, :TOP_K].reshape(n * TOP_K)
    onehot = (flat_e[:, None] == jnp.arange(N_EXPERTS, dtype=jnp.int32)[None, :]).astype(jnp.int32)
    csum = jnp.cumsum(onehot, axis=0)
    counts = csum[-1]
    padded = (counts + tm_blk - 1) // tm_blk * tm_blk
    pad_ends = jnp.cumsum(padded)
    pad_starts = pad_ends - padded
    dest = jnp.sum(onehot * (pad_starts[None, :] + csum - 1), axis=1).astype(jnp.int32)
    n_blocks = -(-(n * TOP_K + N_EXPERTS * (tm_blk - 1)) // tm_blk)
    block_start = jnp.arange(n_blocks, dtype=jnp.int32) * tm_blk
    block_expert = jnp.minimum(jnp.sum(pad_ends[None, :] <= block_start[:, None], axis=1),
                               N_EXPERTS - 1).astype(jnp.int32)
    n_used = (pad_ends[-1] // tm_blk).astype(jnp.int32).reshape(1)
    xs = dispatch(dest, h, n_blocks * tm_blk, tm=tm_tok)
    yb = glu(xs, g, w_gate, w_up, w_down, block_expert, n_used, tm=tm_blk, tf=tf, dense=False)
    return combine(dest, x, gates, yb, tm=tm_tok)


def _final_norm_kernel(x_ref, g_ref, o_ref):
    o_ref[...] = _rms(x_ref[...], g_ref[...])


def final_norm(x, g, *, tm):
    n, d = x.shape
    return pl.pallas_call(
        _final_norm_kernel,
        grid=(n // tm,),
        in_specs=[pl.BlockSpec((tm, d), lambda i: (i, 0)), pl.BlockSpec((1, d), lambda i: (0, 0))],
        out_specs=pl.BlockSpec((tm, d), lambda i: (i, 0)),
        out_shape=jax.ShapeDtypeStruct((n, d), F32),
        compiler_params=_cparams(("parallel",)),
        name="final_norm",
    )(x, g.reshape(1, d))


def _rope_tables(pos):
    half = QK_ROPE_DIM // 2
    inv_freq = ROPE_THETA ** (-jnp.arange(half, dtype=F32) / half)
    ang = pos.astype(F32)[:, None] * inv_freq[None, :]
    cos = jnp.tile(jnp.cos(ang), (1, 2))
    sin = jnp.tile(jnp.sin(ang), (1, 2))
    z = jnp.zeros_like(cos)
    cat = lambda a, b: jnp.concatenate([a, b], axis=1)
    return (cat(cos, z), cat(z, cos), cat(sin, z), cat(z, sin), cat(cos, cos), cat(sin, sin))


def _rot_cols(w):
    half = QK_ROPE_DIM // 2
    return jnp.concatenate([-w[..., half:], w[..., :half]], axis=-1)


def _mla_weights(w_uq, w_dkv):
    ql = w_uq.shape[0]
    wq = w_uq.reshape(ql, MLA_HEADS, QK_NOPE_DIM + QK_ROPE_DIM)
    nope, pe = wq[..., :QK_NOPE_DIM], wq[..., QK_NOPE_DIM:]
    z = jnp.zeros_like(pe)
    even = jnp.concatenate([nope, pe, z], axis=-1)
    odd = jnp.concatenate([nope, z, pe], axis=-1)
    is_even = (jnp.arange(MLA_HEADS) % 2 == 0)[None, :, None]
    w_uq_ext = jnp.where(is_even, even, odd).reshape(ql, MLA_HEADS * Q_HEAD_W).astype(BF16)
    w_uq_rot = _rot_cols(pe).reshape(ql, MLA_HEADS * QK_ROPE_DIM).astype(BF16)
    wc, wp = w_dkv[:, :KV_LORA], w_dkv[:, KV_LORA:]
    wr = _rot_cols(wp)
    w_dkv_ext = jnp.concatenate([wc, wp, wp, wr, wr], axis=1).astype(BF16)
    return w_uq_ext, w_uq_rot, w_dkv_ext


def kernel(x_prompt, x_sample, cache_l1_ckv, cache_l1_kpe, cache_l3_ckv, cache_l3_kpe, cache_mem_k, cache_mem_v, mem_prompt, g_mix, g_xattn, g_mem, g_ffn, g_final, sgu_w_in, sgu_b_in, sgu_ln_g, sgu_ln_b, sgu_w_s, sgu_b_s, sgu_w_out, mla_w_dq, mla_g_q, mla_w_uq, mla_w_dkv, mla_g_kv, mla_w_uk, mla_w_uv, mla_w_o, xa_w_q, xa_w_k, xa_w_v, xa_w_o, ffn_w_gate, ffn_w_up, ffn_w_down, moe_w_router, moe_w_gate, moe_w_up, moe_w_down):
    bp, sp, d = x_prompt.shape
    bs, ss, _ = x_sample.shape
    past = cache_l1_ckv.shape[1]
    depth = g_mix.shape[0]
    n_p, n_s = bp * sp, bs * ss
    mem_tokens = mem_prompt.shape[1]
    xa_w = XA_HEADS * XA_HEAD_DIM
    mla_cache = ((cache_l1_ckv, cache_l1_kpe), (cache_l3_ckv, cache_l3_kpe))

    bf = lambda a: a.astype(BF16)
    xp = x_prompt.reshape(n_p, d)
    xs = x_sample.reshape(n_s, d)
    tm_p, tm_s = min(ROW_TILE, n_p), min(ROW_TILE, n_s)
    tq_p = min(ROW_TILE, sp)
    tf = min(FF_TILE, ffn_w_gate.shape[2])

    mk_all, mv_all = mem_kv(mem_prompt.reshape(bp * mem_tokens, d), g_mem, bf(xa_w_k), bf(xa_w_v),
                            tm=min(ROW_TILE, bp * mem_tokens))
    new_mem_k = mk_all.reshape(depth, bp, mem_tokens, XA_HEADS, XA_HEAD_DIM)
    new_mem_v = mv_all.reshape(depth, bp, mem_tokens, XA_HEADS, XA_HEAD_DIM)

    tabs_p = _rope_tables(jnp.arange(sp, dtype=jnp.int32))
    tabs_s = _rope_tables(past + jnp.arange(ss, dtype=jnp.int32))

    ckv_p, kpe_p, ckv_s, kpe_s, sgu_v_s = [], [], [], [], []
    for layer in range(depth):
        i = layer // 2
        if layer % 2 == 0:
            w_in, w_out = bf(sgu_w_in[i]), bf(sgu_w_out[i])
            sgu_args = (w_in, sgu_b_in[i], sgu_ln_g[i], sgu_ln_b[i])
            zp, _ = sgu_in(xp, g_mix[layer], *sgu_args, tm=tm_p, emit_v=False)
            zs, v_new = sgu_in(xs, g_mix[layer], *sgu_args, tm=tm_s, emit_v=True)
            sgu_v_s.append(v_new.reshape(bs, ss, -1))
            b_col = sgu_b_s[i][:, :, None]
            xp = sgu_mix(xp, zp, sgu_w_s[i], b_col, w_out, tm=MIX_TILE, seg=SGU_CHUNK)
            rep = SGU_CHUNK // ss
            w_s_s = jnp.tile(sgu_w_s[i][:, :ss, :ss], (1, rep, rep))
            b_s_s = jnp.tile(sgu_b_s[i][:, :ss], (1, rep))[:, :, None]
            xs = sgu_mix(xs, zs, w_s_s, b_s_s, w_out, tm=MIX_TILE, seg=ss)
        else:
            w_uq_ext, w_uq_rot, w_dkv_ext = _mla_weights(mla_w_uq[i], mla_w_dkv[i])
            proj_w = (bf(mla_w_dq[i]), mla_g_q[i], w_uq_ext, w_uq_rot, w_dkv_ext, mla_g_kv[i])
            w_uk, w_uv, w_o = bf(mla_w_uk[i]), bf(mla_w_uv[i]), bf(mla_w_o[i])
            q, ckv, ckv_b, kpe = mla_proj(xp, g_mix[layer], *proj_w, tabs_p, tm=tq_p, tab_blocks=sp // tq_p)
            ckv_p.append(ckv.reshape(bp, sp, KV_LORA))
            kpe_p.append(kpe[:, :QK_ROPE_DIM].reshape(bp, sp, QK_ROPE_DIM))
            kn, vv = kv_up(ckv_b, w_uk, w_uv, tm=tm_p)
            o = attention(q.reshape(bp, sp, -1), kn.reshape(bp, sp, -1), bf(kpe).reshape(bp, sp, LANES),
                          vv.reshape(bp, sp, -1), tq=tq_p, tk=tq_p, causal=True)
            xp = proj_res(xp, o.reshape(n_p, -1), w_o, tm=tm_p)
            q, ckv, ckv_b, kpe = mla_proj(xs, g_mix[layer], *proj_w, tabs_s, tm=ss, tab_blocks=1)
            ckv_s.append(ckv.reshape(bs, ss, KV_LORA))
            kpe_s.append(kpe[:, :QK_ROPE_DIM].reshape(bs, ss, QK_ROPE_DIM))
            c_ckv, c_kpe = mla_cache[i]
            kv_len = past + ss
            sk = -(-kv_len // LANES) * LANES
            pad = sk - kv_len
            ckv_all = jnp.concatenate([bf(c_ckv), ckv_b.reshape(bs, ss, KV_LORA),
                                       jnp.zeros((bs, pad, KV_LORA), BF16)], axis=1)
            kpe_all = jnp.concatenate([bf(jnp.tile(c_kpe, (1, 1, 2))), bf(kpe).reshape(bs, ss, LANES),
                                       jnp.zeros((bs, pad, LANES), BF16)], axis=1)
            kn, vv = kv_up(ckv_all.reshape(bs * sk, KV_LORA), w_uk, w_uv, tm=sk // 4)
            o = attention(q.reshape(bs, ss, -1), kn.reshape(bs, sk, -1), kpe_all, vv.reshape(bs, sk, -1),
                          tq=ss, tk=sk, causal=False, kv_len=kv_len)
            xs = proj_res(xs, o.reshape(n_s, -1), w_o, tm=tm_s)

        w_q, w_o = bf(xa_w_q[layer]), bf(xa_w_o[layer])
        xp = xattn(xp.reshape(bp, sp, d), g_xattn[layer], w_q, bf(mk_all[layer]).reshape(bp, mem_tokens, xa_w),
                   bf(mv_all[layer]).reshape(bp, mem_tokens, xa_w), w_o, tm=tq_p).reshape(n_p, d)
        xs = xattn(xs.reshape(bs, ss, d), g_xattn[layer], w_q,
                   bf(cache_mem_k[layer]).reshape(bs, mem_tokens, xa_w),
                   bf(cache_mem_v[layer]).reshape(bs, mem_tokens, xa_w), w_o, tm=ss).reshape(n_s, d)

        if layer % 2 == 0:
            wg, wu, wd = bf(ffn_w_gate[i])[None], bf(ffn_w_up[i])[None], bf(ffn_w_down[i])[None]
            ffn = lambda x, tm: glu(x, g_ffn[layer], wg, wu, wd, jnp.zeros((x.shape[0] // tm,), jnp.int32),
                                    jnp.full((1,), x.shape[0] // tm, jnp.int32), tm=tm, tf=tf, dense=True)
            xp, xs = ffn(xp, tm_p), ffn(xs, tm_s)
        else:
            wr = jnp.pad(moe_w_router[i], ((0, 0), (0, LANES - N_EXPERTS)))
            wg, wu, wd = bf(moe_w_gate[i]), bf(moe_w_up[i]), bf(moe_w_down[i])
            xp = moe(xp, g_ffn[layer], wr, wg, wu, wd, tm_tok=MIX_TILE, tm_blk=tm_p, tf=tf)
            xs = moe(xs, g_ffn[layer], wr, wg, wu, wd, tm_tok=MIX_TILE, tm_blk=min(MIX_TILE, n_s), tf=tf)

    y_prompt = final_norm(xp, g_final, tm=tm_p).reshape(bp, sp, d)
    y_sample = final_norm(xs, g_final, tm=tm_s).reshape(bs, ss, d)
    return (y_prompt, y_sample,
            ckv_p[0], kpe_p[0], ckv_p[1], kpe_p[1],
            new_mem_k, new_mem_v,
            ckv_s[0], kpe_s[0], ckv_s[1], kpe_s[1],
            sgu_v_s[0], sgu_v_s[1])
```

```python
from functools import partial

import numpy as np
import jax
import jax.numpy as jnp
from jax import lax
from jax.experimental import pallas as pl
from jax.experimental.pallas import tpu as pltpu

F32 = jnp.float32
BF16 = jnp.bfloat16

CHUNK = 64
SGU_CHUNK = 128
SGU_GROUPS = 8
MLA_HEADS = 16
QK_NOPE_DIM = 128
QK_ROPE_DIM = 64
V_HEAD_DIM = 128
KV_LORA = 512
ROPE_THETA = 10000.0
XA_HEADS = 4
XA_HEAD_DIM = 128
N_EXPERTS = 8
TOP_K = 2
NORM_EPS = 1e-6

LANES = 128
Q_HEAD_W = 2 * LANES
VMEM_LIMIT = 56 * 1024 * 1024
NEG_BIG = -1e30
ROW_TILE = 512
MIX_TILE = 256
FF_TILE = 512
CAST_BLOCK_ELEMS = 1024 * 1024


def _cparams(sem):
    return pltpu.CompilerParams(dimension_semantics=sem, vmem_limit_bytes=VMEM_LIMIT)


def _rms(x, g):
    return x * lax.rsqrt(jnp.mean(x * x, axis=-1, keepdims=True) + NORM_EPS) * g


def _dot(a, b):
    return jnp.dot(a, b, preferred_element_type=F32)


def _dot_nt(a, b):
    return lax.dot_general(a, b, (((1,), (1,)), ((), ())), preferred_element_type=F32)


def _col_chunks(n, width):
    return [(c, min(width, n - c)) for c in range(0, n, width)]


def _sgu_in_kernel(x_ref, g_ref, w_ref, b_ref, lng_ref, lnb_ref, z_ref, *rest, emit_v):
    if emit_v:
        v_ref, zs_ref = rest
    else:
        (zs_ref,) = rest
    j = pl.program_id(0)
    h = _rms(x_ref[...], g_ref[...]).astype(BF16)
    for c, w in _col_chunks(w_ref.shape[1], 512):
        a = _dot(h, w_ref[:, c:c + w]) + b_ref[:, c:c + w]
        zs_ref[:, c:c + w] = 0.5 * a * (1.0 + lax.erf(a * np.float32(np.sqrt(0.5))))

    @pl.when(j == 0)
    def _():
        z_ref[...] = zs_ref[...].astype(z_ref.dtype)

    @pl.when(j == 1)
    def _():
        z = zs_ref[...]
        mu = jnp.mean(z, axis=-1, keepdims=True)
        zc = z - mu
        var = jnp.mean(zc * zc, axis=-1, keepdims=True)
        v = zc * lax.rsqrt(var + NORM_EPS) * lng_ref[...] + lnb_ref[...]
        z_ref[...] = v.astype(z_ref.dtype)
        if emit_v:
            v_ref[...] = v


def sgu_in(x, g, w_in, b_in, ln_g, ln_b, *, tm, emit_v):
    n, d = x.shape
    ds = w_in.shape[1] // 2
    nt = n // tm
    out_shape = [jax.ShapeDtypeStruct((n, 2 * ds), BF16)]
    out_specs = [pl.BlockSpec((tm, ds), lambda j, i: (i, j))]
    if emit_v:
        out_shape.append(jax.ShapeDtypeStruct((n, ds), F32))
        out_specs.append(pl.BlockSpec((tm, ds), lambda j, i: (i * j, 0)))
    res = pl.pallas_call(
        partial(_sgu_in_kernel, emit_v=emit_v),
        grid=(2, nt),
        in_specs=[
            pl.BlockSpec((tm, d), lambda j, i: (i, 0)),
            pl.BlockSpec((1, d), lambda j, i: (0, 0)),
            pl.BlockSpec((d, ds), lambda j, i: (0, j)),
            pl.BlockSpec((1, ds), lambda j, i: (0, j)),
            pl.BlockSpec((1, ds), lambda j, i: (0, 0)),
            pl.BlockSpec((1, ds), lambda j, i: (0, 0)),
        ],
        out_specs=out_specs,
        out_shape=out_shape,
        scratch_shapes=[pltpu.VMEM((tm, ds), F32)],
        compiler_params=_cparams(("arbitrary", "arbitrary")),
        name="sgu_in",
    )(x, g.reshape(1, d), w_in, b_in.reshape(1, -1), ln_g.reshape(1, ds), ln_b.reshape(1, ds))
    return res if emit_v else (res[0], None)


def _sgu_mix_kernel(x_ref, u_ref, v_ref, ws_ref, bs_ref, wo_ref, o_ref, gs_ref, *, seg):
    tm = x_ref.shape[0]
    gd = v_ref.shape[1] // SGU_GROUPS
    row = lax.broadcasted_iota(jnp.int32, (SGU_CHUNK, SGU_CHUNK), 0)
    col = lax.broadcasted_iota(jnp.int32, (SGU_CHUNK, SGU_CHUNK), 1)
    keep = (col <= row) & ((col // seg) == (row // seg))
    for g in range(SGU_GROUPS):
        wg = jnp.where(keep, ws_ref[g], 0.0).astype(BF16)
        bg = bs_ref[g]
        for c in range(tm // SGU_CHUNK):
            rs = slice(c * SGU_CHUNK, (c + 1) * SGU_CHUNK)
            cs = slice(g * gd, (g + 1) * gd)
            mixed = _dot(wg, v_ref[rs, cs]) + bg
            gs_ref[rs, cs] = (u_ref[rs, cs].astype(F32) * mixed).astype(BF16)
    for c, w in _col_chunks(wo_ref.shape[1], 512):
        o_ref[:, c:c + w] = x_ref[:, c:c + w] + _dot(gs_ref[...], wo_ref[:, c:c + w])


def sgu_mix(x, z, w_s, b_s, w_out, *, tm, seg):
    n, d = x.shape
    ds = z.shape[1] // 2
    return pl.pallas_call(
        partial(_sgu_mix_kernel, seg=seg),
        grid=(n // tm,),
        in_specs=[
            pl.BlockSpec((tm, d), lambda i: (i, 0)),
            pl.BlockSpec((tm, ds), lambda i: (i, 0)),
            pl.BlockSpec((tm, ds), lambda i: (i, 1)),
            pl.BlockSpec(w_s.shape, lambda i: (0, 0, 0)),
            pl.BlockSpec(b_s.shape, lambda i: (0, 0, 0)),
            pl.BlockSpec(w_out.shape, lambda i: (0, 0)),
        ],
        out_specs=pl.BlockSpec((tm, d), lambda i: (i, 0)),
        out_shape=jax.ShapeDtypeStruct((n, d), F32),
        scratch_shapes=[pltpu.VMEM((tm, ds), BF16)],
        compiler_params=_cparams(("parallel",)),
        name="sgu_mix",
    )(x, z, z, w_s, b_s, w_out)


def _mem_kv_kernel(m_ref, g_ref, wk_ref, wv_ref, k_ref, v_ref):
    h = _rms(m_ref[...], g_ref[0]).astype(BF16)
    k_ref[0] = _dot(h, wk_ref[0])
    v_ref[0] = _dot(h, wv_ref[0])


def mem_kv(mem, g_mem, w_k, w_v, *, tm):
    r, d = mem.shape
    nl, _, w = w_k.shape
    return pl.pallas_call(
        _mem_kv_kernel,
        grid=(nl, r // tm),
        in_specs=[
            pl.BlockSpec((tm, d), lambda l, i: (i, 0)),
            pl.BlockSpec((1, 1, d), lambda l, i: (l, 0, 0)),
            pl.BlockSpec((1, d, w), lambda l, i: (l, 0, 0)),
            pl.BlockSpec((1, d, w), lambda l, i: (l, 0, 0)),
        ],
        out_specs=[pl.BlockSpec((1, tm, w), lambda l, i: (l, i, 0))] * 2,
        out_shape=[jax.ShapeDtypeStruct((nl, r, w), F32)] * 2,
        compiler_params=_cparams(("parallel", "parallel")),
        name="mem_kv",
    )(mem, g_mem.reshape(nl, 1, d), w_k, w_v)


def _xattn_kernel(x_ref, g_ref, wq_ref, k_ref, v_ref, wo_ref, o_ref, os_ref):
    x = x_ref[0]
    h = _rms(x, g_ref[...]).astype(BF16)
    q = (_dot(h, wq_ref[...]) * np.float32(XA_HEAD_DIM ** -0.5)).astype(BF16)
    for hd in range(XA_HEADS):
        cs = slice(hd * XA_HEAD_DIM, (hd + 1) * XA_HEAD_DIM)
        s = _dot_nt(q[:, cs], k_ref[0, :, cs])
        p = jnp.exp(s - jnp.max(s, axis=-1, keepdims=True))
        l = jnp.sum(p, axis=-1, keepdims=True)
        os_ref[:, cs] = (_dot(p.astype(BF16), v_ref[0, :, cs]) / l).astype(BF16)
    for c, w in _col_chunks(wo_ref.shape[1], 512):
        o_ref[0, :, c:c + w] = x[:, c:c + w] + _dot(os_ref[...], wo_ref[:, c:c + w])


def xattn(x, g, w_q, mem_k, mem_v, w_o, *, tm):
    b, s, d = x.shape
    m, xw = mem_k.shape[1:]
    return pl.pallas_call(
        _xattn_kernel,
        grid=(b, s // tm),
        in_specs=[
            pl.BlockSpec((1, tm, d), lambda bi, i: (bi, i, 0)),
            pl.BlockSpec((1, d), lambda bi, i: (0, 0)),
            pl.BlockSpec((d, xw), lambda bi, i: (0, 0)),
            pl.BlockSpec((1, m, xw), lambda bi, i: (bi, 0, 0)),
            pl.BlockSpec((1, m, xw), lambda bi, i: (bi, 0, 0)),
            pl.BlockSpec((xw, d), lambda bi, i: (0, 0)),
        ],
        out_specs=pl.BlockSpec((1, tm, d), lambda bi, i: (bi, i, 0)),
        out_shape=jax.ShapeDtypeStruct((b, s, d), F32),
        scratch_shapes=[pltpu.VMEM((tm, xw), BF16)],
        compiler_params=_cparams(("parallel", "parallel")),
        name="xattn",
    )(x, g.reshape(1, d), w_q, mem_k, mem_v, w_o)


def _glu_kernel(be_ref, nb_ref, x_ref, g_ref, wg_ref, wu_ref, wd_ref, o_ref, h_ref, *, dense):
    i = pl.program_id(0)
    f = pl.program_id(1)

    @pl.when(i < nb_ref[0])
    def _():
        @pl.when(f == 0)
        def _():
            if dense:
                x = x_ref[...]
                h_ref[...] = _rms(x, g_ref[...]).astype(BF16)
                o_ref[...] = x
            else:
                h_ref[...] = x_ref[...].astype(BF16)
                o_ref[...] = jnp.zeros_like(o_ref)

        h = h_ref[...]
        a = _dot(h, wg_ref[0])
        b = _dot(h, wu_ref[0])
        hid = (a * jax.nn.sigmoid(a) * b).astype(BF16)
        o_ref[...] += _dot(hid, wd_ref[0])

    @pl.when(jnp.logical_and(i >= nb_ref[0], f == 0))
    def _():
        o_ref[...] = jnp.zeros_like(o_ref)


def glu(x, g, w_gate, w_up, w_down, block_expert, n_blocks_used, *, tm, tf, dense):
    r, d = x.shape
    ff = w_gate.shape[2]
    nf = ff // tf

    def wsel(i, f, be, nb):
        live = i < nb[0]
        return be[i], jnp.where(live, f, nf - 1)

    grid_spec = pltpu.PrefetchScalarGridSpec(
        num_scalar_prefetch=2,
        grid=(r // tm, nf),
        in_specs=[
            pl.BlockSpec((tm, d), lambda i, f, be, nb: (i, 0)),
            pl.BlockSpec((1, d), lambda i, f, be, nb: (0, 0)),
            pl.BlockSpec((1, d, tf), lambda i, f, be, nb: (wsel(i, f, be, nb)[0], 0, wsel(i, f, be, nb)[1])),
            pl.BlockSpec((1, d, tf), lambda i, f, be, nb: (wsel(i, f, be, nb)[0], 0, wsel(i, f, be, nb)[1])),
            pl.BlockSpec((1, tf, d), lambda i, f, be, nb: (wsel(i, f, be, nb)[0], wsel(i, f, be, nb)[1], 0)),
        ],
        out_specs=pl.BlockSpec((tm, d), lambda i, f, be, nb: (i, 0)),
        scratch_shapes=[pltpu.VMEM((tm, d), BF16)],
    )
    return pl.pallas_call(
        partial(_glu_kernel, dense=dense),
        grid_spec=grid_spec,
        out_shape=jax.ShapeDtypeStruct((r, d), F32),
        compiler_params=_cparams(("arbitrary", "arbitrary")),
        name="glu_dense" if dense else "glu_expert",
    )(block_expert, n_blocks_used, x, g.reshape(1, d), w_gate, w_up, w_down)


def _mla_proj_kernel(x_ref, g_ref, wdq_ref, gq_ref, wuq_ref, wrot_ref, wkv_ref, gkv_ref,
                     ce_ref, co_ref, se_ref, so_ref, ck_ref, sk_ref,
                     q_ref, ckv_ref, ckvb_ref, kpe_ref, cq_ref):
    h = _rms(x_ref[...], g_ref[...]).astype(BF16)
    cq_ref[...] = _rms(_dot(h, wdq_ref[...]), gq_ref[...]).astype(BF16)
    kv = _dot(h, wkv_ref[...])
    ckv = _rms(kv[:, :KV_LORA], gkv_ref[...])
    ckv_ref[...] = ckv
    ckvb_ref[...] = ckv.astype(BF16)
    kpe_ref[...] = (kv[:, KV_LORA:KV_LORA + LANES] * ck_ref[...]
                    + kv[:, KV_LORA + LANES:KV_LORA + 2 * LANES] * sk_ref[...])
    scale = np.float32((QK_NOPE_DIM + QK_ROPE_DIM) ** -0.5 * np.log2(np.e))
    cq = cq_ref[...]
    for p in range(MLA_HEADS // 2):
        a = _dot(cq, wuq_ref[:, 2 * p * Q_HEAD_W:(2 * p + 2) * Q_HEAD_W])
        r = _dot(cq, wrot_ref[:, p * LANES:(p + 1) * LANES])
        for k, (c_ref, s_ref) in enumerate(((ce_ref, se_ref), (co_ref, so_ref))):
            q_ref[2 * p + k, :, :LANES] = (a[:, k * Q_HEAD_W:k * Q_HEAD_W + LANES] * scale).astype(BF16)
            pe = a[:, k * Q_HEAD_W + LANES:(k + 1) * Q_HEAD_W] * c_ref[...] + r * s_ref[...]
            q_ref[2 * p + k, :, LANES:] = (pe * scale).astype(BF16)


def mla_proj(x, g, w_dq, g_q, w_uq_ext, w_uq_rot, w_dkv_ext, g_kv, tabs, *, tm, tab_blocks):
    n, d = x.shape
    ql = w_dq.shape[1]
    full = lambda a: pl.BlockSpec(a.shape, lambda i: (0,) * a.ndim)
    tab_spec = pl.BlockSpec((tm, LANES), lambda i: (i % tab_blocks, 0))
    return pl.pallas_call(
        _mla_proj_kernel,
        grid=(n // tm,),
        in_specs=[pl.BlockSpec((tm, d), lambda i: (i, 0)), pl.BlockSpec((1, d), lambda i: (0, 0)),
                  full(w_dq), pl.BlockSpec((1, ql), lambda i: (0, 0)), full(w_uq_ext), full(w_uq_rot),
                  full(w_dkv_ext), pl.BlockSpec((1, KV_LORA), lambda i: (0, 0))] + [tab_spec] * 6,
        out_specs=[pl.BlockSpec((MLA_HEADS, tm, Q_HEAD_W), lambda i: (0, i, 0)),
                   pl.BlockSpec((tm, KV_LORA), lambda i: (i, 0)),
                   pl.BlockSpec((tm, KV_LORA), lambda i: (i, 0)),
                   pl.BlockSpec((tm, LANES), lambda i: (i, 0))],
        out_shape=[jax.ShapeDtypeStruct((MLA_HEADS, n, Q_HEAD_W), BF16),
                   jax.ShapeDtypeStruct((n, KV_LORA), F32),
                   jax.ShapeDtypeStruct((n, KV_LORA), BF16),
                   jax.ShapeDtypeStruct((n, LANES), F32)],
        scratch_shapes=[pltpu.VMEM((tm, ql), BF16)],
        compiler_params=_cparams(("parallel",)),
        name="mla_proj",
    )(x, g.reshape(1, d), w_dq, g_q.reshape(1, ql), w_uq_ext, w_uq_rot, w_dkv_ext,
      g_kv.reshape(1, KV_LORA), *tabs)


def _kv_up_kernel(c_ref, wk_ref, wv_ref, k_ref, v_ref):
    c = c_ref[...]
    for w_ref, o_ref in ((wk_ref, k_ref), (wv_ref, v_ref)):
        for p in range(MLA_HEADS // 2):
            r = _dot(c, w_ref[:, 2 * p * LANES:(2 * p + 2) * LANES]).astype(BF16)
            o_ref[2 * p] = r[:, :LANES]
            o_ref[2 * p + 1] = r[:, LANES:]


def kv_up(ckv, w_uk, w_uv, *, tm):
    n, c = ckv.shape
    w = w_uk.shape[1]
    return pl.pallas_call(
        _kv_up_kernel,
        grid=(n // tm,),
        in_specs=[pl.BlockSpec((tm, c), lambda i: (i, 0)),
                  pl.BlockSpec((c, w), lambda i: (0, 0)),
                  pl.BlockSpec((c, w), lambda i: (0, 0))],
        out_specs=[pl.BlockSpec((MLA_HEADS, tm, LANES), lambda i: (0, i, 0))] * 2,
        out_shape=[jax.ShapeDtypeStruct((MLA_HEADS, n, LANES), BF16)] * 2,
        compiler_params=_cparams(("parallel",)),
        name="kv_up",
    )(ckv, w_uk, w_uv)


def _attn_kernel(qi_ref, ki_ref, last_ref, q_ref, kn_ref, kp_ref, v_ref, o_ref, m_ref, l_ref, acc_ref,
                 *, causal, tq, tk, kv_len):
    step_id = pl.program_id(1)
    qi = qi_ref[step_id]
    ki = ki_ref[step_id]

    @pl.when(ki == 0)
    def _():
        m_ref[...] = jnp.full_like(m_ref, NEG_BIG)
        l_ref[...] = jnp.zeros_like(l_ref)
        acc_ref[...] = jnp.zeros_like(acc_ref)

    def run(masked):
        kp = kp_ref[0]
        keep = None
        if masked:
            qc = lax.broadcasted_iota(jnp.int32, (tq, tk), 0) // CHUNK
            kc = lax.broadcasted_iota(jnp.int32, (tq, tk), 1) // CHUNK
            keep = kc <= qc
        elif kv_len is not None:
            keep = ki * tk + lax.broadcasted_iota(jnp.int32, (tq, tk), 1) < kv_len

        def head(h):
            k = jnp.concatenate([kn_ref[h, 0], kp], axis=1)
            s = _dot_nt(q_ref[h, 0], k)
            if keep is not None:
                s = jnp.where(keep, s, NEG_BIG)
            m_old = m_ref[h]
            m_new = jnp.maximum(m_old, jnp.max(s, axis=-1, keepdims=True))
            alpha = jnp.exp2(m_old - m_new)
            p = jnp.exp2(s - jnp.tile(m_new, (1, tk // LANES)))
            l_ref[h] = alpha * l_ref[h] + jnp.sum(p, axis=-1, keepdims=True)
            acc_ref[h] = alpha * acc_ref[h] + _dot(p.astype(BF16), v_ref[h, 0])
            m_ref[h] = m_new

        def head_pair(j, c):
            head(2 * j)
            head(2 * j + 1)
            return c

        lax.fori_loop(0, MLA_HEADS // 2, head_pair, 0)

    if causal:
        pl.when(ki < qi)(lambda: run(False))
        pl.when(ki == qi)(lambda: run(True))
    else:
        run(False)

    @pl.when(last_ref[step_id] == 1)
    def _():
        for h in range(MLA_HEADS):
            o_ref[0, :, h * V_HEAD_DIM:(h + 1) * V_HEAD_DIM] = (acc_ref[h] / l_ref[h]).astype(o_ref.dtype)


def attention(q, kn, kp, v, *, tq, tk, causal, kv_len=None):
    _, b, sq, _ = q.shape
    sk = kn.shape[2]
    nq, nk = sq // tq, sk // tk
    if causal:
        assert tq == tk and sq == sk and kv_len is None and tq % CHUNK == 0
        pairs = [(qi, ki) for qi in range(nq) for ki in range(qi + 1)]
    else:
        pairs = [(qi, ki) for qi in range(nq) for ki in range(nk)]
    qi_tab = jnp.asarray(np.array([p[0] for p in pairs], np.int32))
    ki_tab = jnp.asarray(np.array([p[1] for p in pairs], np.int32))
    last = [int(i + 1 == len(pairs) or pairs[i + 1][0] != pairs[i][0]) for i in range(len(pairs))]
    last_tab = jnp.asarray(np.array(last, np.int32))
    grid_spec = pltpu.PrefetchScalarGridSpec(
        num_scalar_prefetch=3,
        grid=(b, len(pairs)),
        in_specs=[
            pl.BlockSpec((MLA_HEADS, 1, tq, Q_HEAD_W), lambda bi, s, qt, kt, lt: (0, bi, qt[s], 0)),
            pl.BlockSpec((MLA_HEADS, 1, tk, QK_NOPE_DIM), lambda bi, s, qt, kt, lt: (0, bi, kt[s], 0)),
            pl.BlockSpec((1, tk, LANES), lambda bi, s, qt, kt, lt: (bi, kt[s], 0)),
            pl.BlockSpec((MLA_HEADS, 1, tk, V_HEAD_DIM), lambda bi, s, qt, kt, lt: (0, bi, kt[s], 0)),
        ],
        out_specs=pl.BlockSpec((1, tq, MLA_HEADS * V_HEAD_DIM), lambda bi, s, qt, kt, lt: (bi, qt[s], 0)),
        scratch_shapes=[pltpu.VMEM((MLA_HEADS, tq, LANES), F32), pltpu.VMEM((MLA_HEADS, tq, LANES), F32),
                        pltpu.VMEM((MLA_HEADS, tq, V_HEAD_DIM), F32)],
    )
    return pl.pallas_call(
        partial(_attn_kernel, causal=causal, tq=tq, tk=tk, kv_len=kv_len),
        grid_spec=grid_spec,
        out_shape=jax.ShapeDtypeStruct((b, sq, MLA_HEADS * V_HEAD_DIM), BF16),
        compiler_params=_cparams(("parallel", "arbitrary")),
        name="attention",
    )(qi_tab, ki_tab, last_tab, q, kn, kp, v)


def _proj_res_kernel(x_ref, a_ref, w_ref, o_ref):
    for c, w in _col_chunks(w_ref.shape[1], 512):
        o_ref[:, c:c + w] = x_ref[:, c:c + w] + _dot(a_ref[...], w_ref[:, c:c + w])


def proj_res(x, a, w, *, tm):
    n, d = x.shape
    k = a.shape[1]
    return pl.pallas_call(
        _proj_res_kernel,
        grid=(n // tm,),
        in_specs=[pl.BlockSpec((tm, d), lambda i: (i, 0)),
                  pl.BlockSpec((tm, k), lambda i: (i, 0)),
                  pl.BlockSpec((k, d), lambda i: (0, 0))],
        out_specs=pl.BlockSpec((tm, d), lambda i: (i, 0)),
        out_shape=jax.ShapeDtypeStruct((n, d), F32),
        compiler_params=_cparams(("parallel",)),
        name="proj_res",
    )(x, a, w)


def _route_kernel(x_ref, g_ref, wr_ref, idx_ref, gate_ref):
    h = _rms(x_ref[...], g_ref[...])
    logits =jnp.dot(h, wr_ref[...], preferred_element_type=F32, precision=lax.Precision.HIGHEST)
    lane_i = lax.broadcasted_iota(jnp.int32, logits.shape, 1)
    lane = lane_i.astype(F32)
    logits = jnp.where(lane_i < N_EXPERTS, logits, -jnp.inf)
    m1 = jnp.max(logits, axis=-1, keepdims=True)
    i1 = jnp.min(jnp.where(logits == m1, lane, float(LANES)), axis=-1, keepdims=True)
    rest = jnp.where(lane == i1, -jnp.inf, logits)
    m2 = jnp.max(rest, axis=-1, keepdims=True)
    i2 = jnp.min(jnp.where(rest == m2, lane, float(LANES)), axis=-1, keepdims=True)
    e2 = jnp.exp(m2 - m1)
    g1 = 1.0 / (1.0 + e2)
    g2 = e2 / (1.0 + e2)
    idx_ref[...] = jnp.where(lane_i == 0, i1, jnp.where(lane_i == 1, i2, 0.0)).astype(jnp.int32)
    gate_ref[...] = jnp.where(lane_i == 0, g1, jnp.where(lane_i == 1, g2, 0.0))


def route(x, g, w_router_pad, *, tm):
    n, d = x.shape
    return pl.pallas_call(
        _route_kernel,
        grid=(n // tm,),
        in_specs=[pl.BlockSpec((tm, d), lambda i: (i, 0)),
                  pl.BlockSpec((1, d), lambda i: (0, 0)),
                  pl.BlockSpec((d, LANES), lambda i: (0, 0))],
        out_specs=[pl.BlockSpec((tm, LANES), lambda i: (i, 0)),
                   pl.BlockSpec((tm, LANES), lambda i: (i, 0))],
        out_shape=[jax.ShapeDtypeStruct((n, LANES), jnp.int32),
                   jax.ShapeDtypeStruct((n, LANES), F32)],
        compiler_params=_cparams(("parallel",)),
        name="moe_route",
    )(x, g.reshape(1, d), w_router_pad)


def _row_copy(src_ref, src_row, dst_ref, dst_row, sem):
    return pltpu.make_async_copy(src_ref.at[pl.ds(src_row, 1)], dst_ref.at[pl.ds(dst_row, 1)], sem)


def _dispatch_kernel(dest_ref, x_ref, g_ref, zero_ref, xs_ref, h_ref, sem, *, tm):
    del zero_ref
    base = pl.program_id(0) * tm
    h_ref[...] = _rms(x_ref[...], g_ref[...])

    def issue(t, c):
        for k in range(TOP_K):
            _row_copy(h_ref, t, xs_ref, dest_ref[TOP_K * (base + t) + k], sem).start()
        return c

    lax.fori_loop(0, tm, issue, 0, unroll=8)

    def drain(t, c):
        for k in range(TOP_K):
            _row_copy(h_ref, 0, xs_ref, 0, sem).wait()
        return c

    lax.fori_loop(0, tm, drain, 0, unroll=8)


def dispatch(dest, x, g, n_slots, *, tm):
    n, d = x.shape
    zeros = jnp.zeros((n_slots, d), F32)
    grid_spec = pltpu.PrefetchScalarGridSpec(
        num_scalar_prefetch=1,
        grid=(n // tm,),
        in_specs=[pl.BlockSpec((tm, d), lambda i, ds: (i, 0)),
                  pl.BlockSpec((1, d), lambda i, ds: (0, 0)),
                  pl.BlockSpec(memory_space=pl.ANY)],
        out_specs=pl.BlockSpec(memory_space=pl.ANY),
        scratch_shapes=[pltpu.VMEM((tm, d), F32), pltpu.SemaphoreType.DMA],
    )
    return pl.pallas_call(
        partial(_dispatch_kernel, tm=tm),
        grid_spec=grid_spec,
        out_shape=jax.ShapeDtypeStruct((n_slots, d), F32),
        input_output_aliases={3: 0},
        compiler_params=_cparams(("arbitrary",)),
        name="moe_dispatch",
    )(dest, x, g.reshape(1, d), zeros)


def _combine_kernel(dest_ref, x_ref, gate_ref, yb_ref, o_ref, y0_ref, y1_ref, sem, *, tm):
    base = pl.program_id(0) * tm
    bufs = (y0_ref, y1_ref)

    def issue(t, c):
        for k in range(TOP_K):
            _row_copy(yb_ref, dest_ref[TOP_K * (base + t) + k], bufs[k], t, sem).start()
        return c

    lax.fori_loop(0, tm, issue, 0, unroll=8)

    def drain(t, c):
        for k in range(TOP_K):
            _row_copy(yb_ref, 0, bufs[k], 0, sem).wait()
        return c

    lax.fori_loop(0, tm, drain, 0, unroll=8)
    g0 = gate_ref[:, 0:1]
    g1 = gate_ref[:, 1:2]
    o_ref[...] = x_ref[...] + (g0 * y0_ref[...] + g1 * y1_ref[...])


def combine(dest, x, gates, yb, *, tm):
    n, d = x.shape
    grid_spec = pltpu.PrefetchScalarGridSpec(
        num_scalar_prefetch=1,
        grid=(n // tm,),
        in_specs=[pl.BlockSpec((tm, d), lambda i, ds: (i, 0)),
                  pl.BlockSpec((tm, LANES), lambda i, ds: (i, 0)),
                  pl.BlockSpec(memory_space=pl.ANY)],
        out_specs=pl.BlockSpec((tm, d), lambda i, ds: (i, 0)),
        scratch_shapes=[pltpu.VMEM((tm, d), F32), pltpu.VMEM((tm, d), F32), pltpu.SemaphoreType.DMA],
    )
    return pl.pallas_call(
        partial(_combine_kernel, tm=tm),
        grid_spec=grid_spec,
        out_shape=jax.ShapeDtypeStruct((n, d), F32),
        compiler_params=_cparams(("arbitrary",)),
        name="moe_combine",
    )(dest, x, gates, yb)


def moe(x, g, w_router_pad, w_gate, w_up, w_down, *, tm_tok, tm_blk, tf):
    n, d = x.shape
    idx, gates = route(x, g, w_router_pad, tm=tm_tok)
    flat_e = idx[:, :TOP_K].reshape(n * TOP_K)
    onehot = (flat_e[:, None] == jnp.arange(N_EXPERTS, dtype=jnp.int32)[None, :]).astype(jnp.int32)
    csum = jnp.cumsum(onehot, axis=0)
    counts = csum[-1]
    padded = (counts + tm_blk - 1) // tm_blk * tm_blk
    pad_ends = jnp.cumsum(padded)
    pad_starts = pad_ends - padded
    dest = jnp.sum(onehot * (pad_starts[None, :] + csum - 1), axis=1).astype(jnp.int32)
    n_blocks = -(-(n * TOP_K + N_EXPERTS * (tm_blk - 1)) // tm_blk)
    block_start = jnp.arange(n_blocks, dtype=jnp.int32) * tm_blk
    block_expert = jnp.minimum(jnp.sum(pad_ends[None, :] <= block_start[:, None], axis=1),
                               N_EXPERTS - 1).astype(jnp.int32)
    n_used = (pad_ends[-1] // tm_blk).astype(jnp.int32).reshape(1)
    xs = dispatch(dest, x, g, n_blocks * tm_blk, tm=tm_tok)
    yb = glu(xs, g, w_gate, w_up, w_down, block_expert, n_used, tm=tm_blk, tf=tf, dense=False)
    return combine(dest, x, gates, yb, tm=tm_tok)


def _cast_kernel(w_ref, o_ref):
    o_ref[...] = w_ref[...].astype(o_ref.dtype)


def cast_group(w, first, count):
    _, r, c = w.shape
    tr = r
    while tr * c > CAST_BLOCK_ELEMS and tr % 32 == 0:
        tr //= 2
    return pl.pallas_call(
        _cast_kernel,
        grid=(count, r // tr),
        in_specs=[pl.BlockSpec((1, tr, c), lambda e, i: (first + e, i, 0))],
        out_specs=pl.BlockSpec((1, tr, c), lambda e, i: (e, i, 0)),
        out_shape=jax.ShapeDtypeStruct((count, r, c), BF16),
        compiler_params=_cparams(("parallel", "parallel")),
        name="cast_bf16",
    )(w)


def _final_norm_kernel(x_ref, g_ref, o_ref):
    o_ref[...] = _rms(x_ref[...], g_ref[...])


def final_norm(x, g, *, tm):
    n, d = x.shape
    return pl.pallas_call(
        _final_norm_kernel,
        grid=(n // tm,),
        in_specs=[pl.BlockSpec((tm, d), lambda i: (i, 0)), pl.BlockSpec((1, d), lambda i: (0, 0))],
        out_specs=pl.BlockSpec((tm, d), lambda i: (i, 0)),
        out_shape=jax.ShapeDtypeStruct((n, d), F32),
        compiler_params=_cparams(("parallel",)),
        name="final_norm",
    )(x, g.reshape(1, d))


def _rope_tables(pos):
    half = QK_ROPE_DIM // 2
    inv_freq = ROPE_THETA ** (-jnp.arange(half, dtype=F32) / half)
    ang = pos.astype(F32)[:, None] * inv_freq[None, :]
    cos = jnp.tile(jnp.cos(ang), (1, 2))
    sin = jnp.tile(jnp.sin(ang), (1, 2))
    z = jnp.zeros_like(cos)
    cat = lambda a, b: jnp.concatenate([a, b], axis=1)
    return (cat(cos, z), cat(z, cos), cat(sin, z), cat(z, sin), cat(cos, cos), cat(sin, sin))


def _rot_cols(w):
    half = QK_ROPE_DIM // 2
    return jnp.concatenate([-w[..., half:], w[..., :half]], axis=-1)


def _mla_weights(w_uq, w_dkv):
    ql = w_uq.shape[0]
    wq = w_uq.reshape(ql, MLA_HEADS, QK_NOPE_DIM + QK_ROPE_DIM)
    nope, pe = wq[..., :QK_NOPE_DIM], wq[..., QK_NOPE_DIM:]
    z = jnp.zeros_like(pe)
    even = jnp.concatenate([nope, pe, z], axis=-1)
    odd = jnp.concatenate([nope, z, pe], axis=-1)
    is_even = (jnp.arange(MLA_HEADS) % 2 == 0)[None, :, None]
    w_uq_ext = jnp.where(is_even, even, odd).reshape(ql, MLA_HEADS * Q_HEAD_W).astype(BF16)
    w_uq_rot = _rot_cols(pe).reshape(ql, MLA_HEADS * QK_ROPE_DIM).astype(BF16)
    wc, wp = w_dkv[:, :KV_LORA], w_dkv[:, KV_LORA:]
    wr = _rot_cols(wp)
    w_dkv_ext = jnp.concatenate([wc, wp, wp, wr, wr], axis=1).astype(BF16)
    return w_uq_ext, w_uq_rot, w_dkv_ext


def kernel(x_prompt, x_sample, cache_l1_ckv, cache_l1_kpe, cache_l3_ckv, cache_l3_kpe, cache_mem_k, cache_mem_v, mem_prompt, g_mix, g_xattn, g_mem, g_ffn, g_final, sgu_w_in, sgu_b_in, sgu_ln_g, sgu_ln_b, sgu_w_s, sgu_b_s, sgu_w_out, mla_w_dq, mla_g_q, mla_w_uq, mla_w_dkv, mla_g_kv, mla_w_uk, mla_w_uv, mla_w_o, xa_w_q, xa_w_k, xa_w_v, xa_w_o, ffn_w_gate, ffn_w_up, ffn_w_down, moe_w_router, moe_w_gate, moe_w_up, moe_w_down):
    bp, sp, d = x_prompt.shape
    bs, ss, _ = x_sample.shape
    past = cache_l1_ckv.shape[1]
    depth = g_mix.shape[0]
    n_p, n_s = bp * sp, bs * ss
    mem_tokens = mem_prompt.shape[1]
    xa_w = XA_HEADS * XA_HEAD_DIM
    mla_cache = ((cache_l1_ckv, cache_l1_kpe), (cache_l3_ckv, cache_l3_kpe))

    bf = lambda a: a.astype(BF16)
    xp = x_prompt.reshape(n_p, d)
    xs = x_sample.reshape(n_s, d)
    tm_p, tm_s = min(ROW_TILE, n_p), min(ROW_TILE, n_s)
    tq_p = min(ROW_TILE, sp)
    tf = min(FF_TILE, ffn_w_gate.shape[2])

    mk_all, mv_all = mem_kv(mem_prompt.reshape(bp * mem_tokens, d), g_mem, bf(xa_w_k), bf(xa_w_v),
                            tm=min(ROW_TILE, bp * mem_tokens))
    new_mem_k = mk_all.reshape(depth, bp, mem_tokens, XA_HEADS, XA_HEAD_DIM)
    new_mem_v = mv_all.reshape(depth, bp, mem_tokens, XA_HEADS, XA_HEAD_DIM)

    tabs_p = _rope_tables(jnp.arange(sp, dtype=jnp.int32))
    tabs_s = _rope_tables(past + jnp.arange(ss, dtype=jnp.int32))

    ckv_p, kpe_p, ckv_s, kpe_s, sgu_v_s = [], [], [], [], []
    for layer in range(depth):
        i = layer // 2
        if layer % 2 == 0:
            w_in, w_out = bf(sgu_w_in[i]), bf(sgu_w_out[i])
            sgu_args = (w_in, sgu_b_in[i], sgu_ln_g[i], sgu_ln_b[i])
            zp, _ = sgu_in(xp, g_mix[layer], *sgu_args, tm=tm_p, emit_v=False)
            zs, v_new = sgu_in(xs, g_mix[layer], *sgu_args, tm=tm_s, emit_v=True)
            sgu_v_s.append(v_new.reshape(bs, ss, -1))
            b_col = sgu_b_s[i][:, :, None]
            xp = sgu_mix(xp, zp, sgu_w_s[i], b_col, w_out, tm=MIX_TILE, seg=SGU_CHUNK)
            rep = SGU_CHUNK // ss
            w_s_s = jnp.tile(sgu_w_s[i][:, :ss, :ss], (1, rep, rep))
            b_s_s = jnp.tile(sgu_b_s[i][:, :ss], (1, rep))[:, :, None]
            xs = sgu_mix(xs, zs, w_s_s, b_s_s, w_out, tm=MIX_TILE, seg=ss)
        else:
            w_uq_ext, w_uq_rot, w_dkv_ext = _mla_weights(mla_w_uq[i], mla_w_dkv[i])
            proj_w = (bf(mla_w_dq[i]), mla_g_q[i], w_uq_ext, w_uq_rot, w_dkv_ext, mla_g_kv[i])
            w_uk, w_uv, w_o = bf(mla_w_uk[i]), bf(mla_w_uv[i]), bf(mla_w_o[i])
            q, ckv, ckv_b, kpe = mla_proj(xp, g_mix[layer], *proj_w, tabs_p, tm=tq_p, tab_blocks=sp // tq_p)
            ckv_p.append(ckv.reshape(bp, sp, KV_LORA))
            kpe_p.append(kpe[:, :QK_ROPE_DIM].reshape(bp, sp, QK_ROPE_DIM))
            kn, vv = kv_up(ckv_b, w_uk, w_uv, tm=tm_p)
            hm = lambda a, b, s: a.reshape(MLA_HEADS, b, s, a.shape[-1])
            o = attention(hm(q, bp, sp), hm(kn, bp, sp), bf(kpe).reshape(bp, sp, LANES), hm(vv, bp, sp),
                          tq=tq_p, tk=tq_p, causal=True)
            xp = proj_res(xp, o.reshape(n_p, -1), w_o, tm=tm_p)
            q, ckv, ckv_b, kpe = mla_proj(xs, g_mix[layer], *proj_w, tabs_s, tm=ss, tab_blocks=1)
            ckv_s.append(ckv.reshape(bs, ss, KV_LORA))
            kpe_s.append(kpe[:, :QK_ROPE_DIM].reshape(bs, ss, QK_ROPE_DIM))
            c_ckv, c_kpe = mla_cache[i]
            kv_len = past + ss
            sk = -(-kv_len // LANES) * LANES
            pad = sk - kv_len
            ckv_all = jnp.concatenate([bf(c_ckv), ckv_b.reshape(bs, ss, KV_LORA),
                                       jnp.zeros((bs, pad, KV_LORA), BF16)], axis=1)
            kpe_all = jnp.concatenate([bf(jnp.tile(c_kpe, (1, 1, 2))), bf(kpe).reshape(bs, ss, LANES),
                                       jnp.zeros((bs, pad, LANES), BF16)], axis=1)
            kn, vv = kv_up(ckv_all.reshape(bs * sk, KV_LORA), w_uk, w_uv, tm=sk // 4)
            o = attention(hm(q, bs, ss), hm(kn, bs, sk), kpe_all, hm(vv, bs, sk),
                          tq=ss, tk=sk, causal=False, kv_len=kv_len)
            xs = proj_res(xs, o.reshape(n_s, -1), w_o, tm=tm_s)

        w_q, w_o = bf(xa_w_q[layer]), bf(xa_w_o[layer])
        xp = xattn(xp.reshape(bp, sp, d), g_xattn[layer], w_q, bf(mk_all[layer]).reshape(bp, mem_tokens, xa_w),
                   bf(mv_all[layer]).reshape(bp, mem_tokens, xa_w), w_o, tm=tq_p).reshape(n_p, d)
        xs = xattn(xs.reshape(bs, ss, d), g_xattn[layer], w_q,
                   bf(cache_mem_k[layer]).reshape(bs, mem_tokens, xa_w),
                   bf(cache_mem_v[layer]).reshape(bs, mem_tokens, xa_w), w_o, tm=ss).reshape(n_s, d)

        if layer % 2 == 0:
            wg, wu, wd = (cast_group(w, i, 1) for w in (ffn_w_gate, ffn_w_up, ffn_w_down))
            ffn = lambda x, tm: glu(x, g_ffn[layer], wg, wu, wd, jnp.zeros((x.shape[0] // tm,), jnp.int32),
                                    jnp.full((1,), x.shape[0] // tm, jnp.int32), tm=tm, tf=tf, dense=True)
            xp, xs = ffn(xp, tm_p), ffn(xs, tm_s)
        else:
            wr = jnp.pad(moe_w_router[i], ((0, 0), (0, LANES - N_EXPERTS)))
            wg, wu, wd = (cast_group(w.reshape((-1,) + w.shape[2:]), i * N_EXPERTS, N_EXPERTS)
                          for w in (moe_w_gate, moe_w_up, moe_w_down))
            xp = moe(xp, g_ffn[layer], wr, wg, wu, wd, tm_tok=MIX_TILE, tm_blk=tm_p, tf=tf)
            xs = moe(xs, g_ffn[layer], wr, wg, wu, wd, tm_tok=MIX_TILE, tm_blk=min(MIX_TILE, n_s), tf=tf)

    y_prompt = final_norm(xp, g_final, tm=tm_p).reshape(bp, sp, d)
    y_sample = final_norm(xs, g_final, tm=tm_s).reshape(bs, ss, d)
    return (y_prompt, y_sample,
            ckv_p[0], kpe_p[0], ckv_p[1], kpe_p[1],
            new_mem_k, new_mem_v,
            ckv_s[0], kpe_s[0], ckv_s[1], kpe_s[1],
            sgu_v_s[0], sgu_v_s[1])
```

```python
from functools import partial

import numpy as np
import jax
import jax.numpy as jnp
from jax import lax
from jax.experimental import pallas as pl
from jax.experimental.pallas import tpu as pltpu

F32 = jnp.float32
BF16 = jnp.bfloat16

CHUNK = 64
SGU_CHUNK = 128
SGU_GROUPS = 8
MLA_HEADS = 16
QK_NOPE_DIM = 128
QK_ROPE_DIM = 64
V_HEAD_DIM = 128
KV_LORA = 512
ROPE_THETA = 10000.0
XA_HEADS = 4
XA_HEAD_DIM = 128
N_EXPERTS = 8
TOP_K = 2
NORM_EPS = 1e-6

LANES = 128
Q_HEAD_W = 2 * LANES
VMEM_LIMIT = 56 * 1024 * 1024
NEG_BIG = -1e30
ROW_TILE = 512
MIX_TILE = 256
FF_TILE = 1024
CAST_BLOCK_ELEMS = 1024 * 1024
HEADS_PER_ITER = 16


def _cparams(sem):
    return pltpu.CompilerParams(dimension_semantics=sem, vmem_limit_bytes=VMEM_LIMIT)


def _rms(x, g):
    return x * lax.rsqrt(jnp.mean(x * x, axis=-1, keepdims=True) + NORM_EPS) * g


def _dot(a, b):
    return jnp.dot(a, b, preferred_element_type=F32)


def _dot_nt(a, b):
    return lax.dot_general(a, b, (((1,), (1,)), ((), ())), preferred_element_type=F32)


def _col_chunks(n, width):
    return [(c, min(width, n - c)) for c in range(0, n, width)]


def _sgu_in_kernel(x_ref, g_ref, w_ref, b_ref, lng_ref, lnb_ref, z_ref, *rest, emit_v):
    if emit_v:
        v_ref, zs_ref = rest
    else:
        (zs_ref,) = rest
    j = pl.program_id(0)
    h = _rms(x_ref[...], g_ref[...]).astype(BF16)
    for c, w in _col_chunks(w_ref.shape[1], 512):
        a = _dot(h, w_ref[:, c:c + w]) + b_ref[:, c:c + w]
        zs_ref[:, c:c + w] = 0.5 * a * (1.0 + lax.erf(a * np.float32(np.sqrt(0.5))))

    @pl.when(j == 0)
    def _():
        z_ref[...] = zs_ref[...].astype(z_ref.dtype)

    @pl.when(j == 1)
    def _():
        z = zs_ref[...]
        mu = jnp.mean(z, axis=-1, keepdims=True)
        zc = z - mu
        var = jnp.mean(zc * zc, axis=-1, keepdims=True)
        v = zc * lax.rsqrt(var + NORM_EPS) * lng_ref[...] + lnb_ref[...]
        z_ref[...] = v.astype(z_ref.dtype)
        if emit_v:
            v_ref[...] = v


def sgu_in(x, g, w_in, b_in, ln_g, ln_b, *, tm, emit_v):
    n, d = x.shape
    ds = w_in.shape[1] // 2
    nt = n // tm
    out_shape = [jax.ShapeDtypeStruct((n, 2 * ds), BF16)]
    out_specs = [pl.BlockSpec((tm, ds), lambda j, i: (i, j))]
    if emit_v:
        out_shape.append(jax.ShapeDtypeStruct((n, ds), F32))
        out_specs.append(pl.BlockSpec((tm, ds), lambda j, i: (i * j, 0)))
    res = pl.pallas_call(
        partial(_sgu_in_kernel, emit_v=emit_v),
        grid=(2, nt),
        in_specs=[
            pl.BlockSpec((tm, d), lambda j, i: (i, 0)),
            pl.BlockSpec((1, d), lambda j, i: (0, 0)),
            pl.BlockSpec((d, ds), lambda j, i: (0, j)),
            pl.BlockSpec((1, ds), lambda j, i: (0, j)),
            pl.BlockSpec((1, ds), lambda j, i: (0, 0)),
            pl.BlockSpec((1, ds), lambda j, i: (0, 0)),
        ],
        out_specs=out_specs,
        out_shape=out_shape,
        scratch_shapes=[pltpu.VMEM((tm, ds), F32)],
        compiler_params=_cparams(("arbitrary", "arbitrary")),
        name="sgu_in",
    )(x, g.reshape(1, d), w_in, b_in.reshape(1, -1), ln_g.reshape(1, ds), ln_b.reshape(1, ds))
    return res if emit_v else (res[0], None)


def _sgu_mix_kernel(x_ref, u_ref, v_ref, ws_ref, bs_ref, wo_ref, o_ref, gs_ref, *, seg):
    tm = x_ref.shape[0]
    gd = v_ref.shape[1] // SGU_GROUPS
    row = lax.broadcasted_iota(jnp.int32, (SGU_CHUNK, SGU_CHUNK), 0)
    col = lax.broadcasted_iota(jnp.int32, (SGU_CHUNK, SGU_CHUNK), 1)
    keep = (col <= row) & ((col // seg) == (row // seg))
    for g in range(SGU_GROUPS):
        wg = jnp.where(keep, ws_ref[g], 0.0).astype(BF16)
        bg = bs_ref[g]
        for c in range(tm // SGU_CHUNK):
            rs = slice(c * SGU_CHUNK, (c + 1) * SGU_CHUNK)
            cs = slice(g * gd, (g + 1) * gd)
            mixed = _dot(wg, v_ref[rs, cs]) + bg
            gs_ref[rs, cs] = (u_ref[rs, cs].astype(F32) * mixed).astype(BF16)
    for c, w in _col_chunks(wo_ref.shape[1], 512):
        o_ref[:, c:c + w] = x_ref[:, c:c + w] + _dot(gs_ref[...], wo_ref[:, c:c + w])


def sgu_mix(x, z, w_s, b_s, w_out, *, tm, seg):
    n, d = x.shape
    ds = z.shape[1] // 2
    return pl.pallas_call(
        partial(_sgu_mix_kernel, seg=seg),
        grid=(n // tm,),
        in_specs=[
            pl.BlockSpec((tm, d), lambda i: (i, 0)),
            pl.BlockSpec((tm, ds), lambda i: (i, 0)),
            pl.BlockSpec((tm, ds), lambda i: (i, 1)),
            pl.BlockSpec(w_s.shape, lambda i: (0, 0, 0)),
            pl.BlockSpec(b_s.shape, lambda i: (0, 0, 0)),
            pl.BlockSpec(w_out.shape, lambda i: (0, 0)),
        ],
        out_specs=pl.BlockSpec((tm, d), lambda i: (i, 0)),
        out_shape=jax.ShapeDtypeStruct((n, d), F32),
        scratch_shapes=[pltpu.VMEM((tm, ds), BF16)],
        compiler_params=_cparams(("parallel",)),
        name="sgu_mix",
    )(x, z, z, w_s, b_s, w_out)


def _mem_kv_kernel(m_ref, g_ref, wk_ref, wv_ref, k_ref, v_ref):
    h = _rms(m_ref[...], g_ref[0]).astype(BF16)
    k_ref[0] = _dot(h, wk_ref[0])
    v_ref[0] = _dot(h, wv_ref[0])


def mem_kv(mem, g_mem, w_k, w_v, *, tm):
    r, d = mem.shape
    nl, _, w = w_k.shape
    return pl.pallas_call(
        _mem_kv_kernel,
        grid=(nl, r // tm),
        in_specs=[
            pl.BlockSpec((tm, d), lambda l, i: (i, 0)),
            pl.BlockSpec((1, 1, d), lambda l, i: (l, 0, 0)),
            pl.BlockSpec((1, d, w), lambda l, i: (l, 0, 0)),
            pl.BlockSpec((1, d, w), lambda l, i: (l, 0, 0)),
        ],
        out_specs=[pl.BlockSpec((1, tm, w), lambda l, i: (l, i, 0))] * 2,
        out_shape=[jax.ShapeDtypeStruct((nl, r, w), F32)] * 2,
        compiler_params=_cparams(("parallel", "parallel")),
        name="mem_kv",
    )(mem, g_mem.reshape(nl, 1, d), w_k, w_v)


def _xattn_kernel(x_ref, g_ref, wq_ref, k_ref, v_ref, wo_ref, o_ref, os_ref):
    x = x_ref[0]
    h = _rms(x, g_ref[...]).astype(BF16)
    q = (_dot(h, wq_ref[...]) * np.float32(XA_HEAD_DIM ** -0.5)).astype(BF16)
    for hd in range(XA_HEADS):
        cs = slice(hd * XA_HEAD_DIM, (hd + 1) * XA_HEAD_DIM)
        s = _dot_nt(q[:, cs], k_ref[0, :, cs])
        p = jnp.exp(s - jnp.max(s, axis=-1, keepdims=True))
        l = jnp.sum(p, axis=-1, keepdims=True)
        os_ref[:, cs] = (_dot(p.astype(BF16), v_ref[0, :, cs]) / l).astype(BF16)
    for c, w in _col_chunks(wo_ref.shape[1], 512):
        o_ref[0, :, c:c + w] = x[:, c:c + w] + _dot(os_ref[...], wo_ref[:, c:c + w])


def xattn(x, g, w_q, mem_k, mem_v, w_o, *, tm):
    b, s, d = x.shape
    m, xw = mem_k.shape[1:]
    return pl.pallas_call(
        _xattn_kernel,
        grid=(b, s // tm),
        in_specs=[
            pl.BlockSpec((1, tm, d), lambda bi, i: (bi, i, 0)),
            pl.BlockSpec((1, d), lambda bi, i: (0, 0)),
            pl.BlockSpec((d, xw), lambda bi, i: (0, 0)),
            pl.BlockSpec((1, m, xw), lambda bi, i: (bi, 0, 0)),
            pl.BlockSpec((1, m, xw), lambda bi, i: (bi, 0, 0)),
            pl.BlockSpec((xw, d), lambda bi, i: (0, 0)),
        ],
        out_specs=pl.BlockSpec((1, tm, d), lambda bi, i: (bi, i, 0)),
        out_shape=jax.ShapeDtypeStruct((b, s, d), F32),
        scratch_shapes=[pltpu.VMEM((tm, xw), BF16)],
        compiler_params=_cparams(("parallel", "parallel")),
        name="xattn",
    )(x, g.reshape(1, d), w_q, mem_k, mem_v, w_o)


def _glu_kernel(be_ref, nb_ref, x_ref, g_ref, wg_ref, wu_ref, wd_ref, o_ref, h_ref, *, dense):
    i = pl.program_id(0)
    f = pl.program_id(1)

    @pl.when(i < nb_ref[0])
    def _():
        @pl.when(f == 0)
        def _():
            if dense:
                x = x_ref[...]
                h_ref[...] = _rms(x, g_ref[...]).astype(BF16)
                o_ref[...] = x
            else:
                h_ref[...] = x_ref[...].astype(BF16)
                o_ref[...] = jnp.zeros_like(o_ref)

        h = h_ref[...]
        a = _dot(h, wg_ref[0])
        b = _dot(h, wu_ref[0])
        hid = (a * jax.nn.sigmoid(a) * b).astype(BF16)
        o_ref[...] += _dot(hid, wd_ref[0])

    @pl.when(jnp.logical_and(i >= nb_ref[0], f == 0))
    def _():
        o_ref[...] = jnp.zeros_like(o_ref)


def glu(x, g, w_gate, w_up, w_down, block_expert, n_blocks_used, *, tm, tf, dense):
    r, d = x.shape
    ff = w_gate.shape[2]
    nf = ff // tf

    def wsel(i, f, be, nb):
        live = i < nb[0]
        return be[i], jnp.where(live, f, nf - 1)

    grid_spec = pltpu.PrefetchScalarGridSpec(
        num_scalar_prefetch=2,
        grid=(r // tm, nf),
        in_specs=[
            pl.BlockSpec((tm, d), lambda i, f, be, nb: (i, 0)),
            pl.BlockSpec((1, d), lambda i, f, be, nb: (0, 0)),
            pl.BlockSpec((1, d, tf), lambda i, f, be, nb: (wsel(i, f, be, nb)[0], 0, wsel(i, f, be, nb)[1])),
            pl.BlockSpec((1, d, tf), lambda i, f, be, nb: (wsel(i, f, be, nb)[0], 0, wsel(i, f, be, nb)[1])),
            pl.BlockSpec((1, tf, d), lambda i, f, be, nb: (wsel(i, f, be, nb)[0], wsel(i, f, be, nb)[1], 0)),
        ],
        out_specs=pl.BlockSpec((tm, d), lambda i, f, be, nb: (i, 0)),
        scratch_shapes=[pltpu.VMEM((tm, d), BF16)],
    )
    return pl.pallas_call(
        partial(_glu_kernel, dense=dense),
        grid_spec=grid_spec,
        out_shape=jax.ShapeDtypeStruct((r, d), F32),
        compiler_params=_cparams(("arbitrary", "arbitrary")),
        name="glu_dense" if dense else "glu_expert",
    )(block_expert, n_blocks_used, x, g.reshape(1, d), w_gate, w_up, w_down)


def _mla_proj_kernel(x_ref, g_ref, wdq_ref, gq_ref, wuq_ref, wrot_ref, wkv_ref, gkv_ref,
                     ce_ref, co_ref, se_ref, so_ref, ck_ref, sk_ref,
                     q_ref, ckv_ref, ckvb_ref, kpe_ref, cq_ref):
    h = _rms(x_ref[...], g_ref[...]).astype(BF16)
    cq_ref[...] = _rms(_dot(h, wdq_ref[...]), gq_ref[...]).astype(BF16)
    kv = _dot(h, wkv_ref[...])
    ckv = _rms(kv[:, :KV_LORA], gkv_ref[...])
    ckv_ref[...] = ckv
    ckvb_ref[...] = ckv.astype(BF16)
    kpe_ref[...] = (kv[:, KV_LORA:KV_LORA + LANES] * ck_ref[...]
                    + kv[:, KV_LORA + LANES:KV_LORA + 2 * LANES] * sk_ref[...])
    scale = np.float32((QK_NOPE_DIM + QK_ROPE_DIM) ** -0.5 * np.log2(np.e))
    cq = cq_ref[...]
    for p in range(MLA_HEADS // 2):
        a = _dot(cq, wuq_ref[:, 2 * p * Q_HEAD_W:(2 * p + 2) * Q_HEAD_W])
        r = _dot(cq, wrot_ref[:, p * LANES:(p + 1) * LANES])
        for k, (c_ref, s_ref) in enumerate(((ce_ref, se_ref), (co_ref, so_ref))):
            q_ref[2 * p + k, :, :LANES] = (a[:, k * Q_HEAD_W:k * Q_HEAD_W + LANES] * scale).astype(BF16)
            pe = a[:, k * Q_HEAD_W + LANES:(k + 1) * Q_HEAD_W] * c_ref[...] + r * s_ref[...]
            q_ref[2 * p + k, :, LANES:] = (pe * scale).astype(BF16)


def mla_proj(x, g, w_dq, g_q, w_uq_ext, w_uq_rot, w_dkv_ext, g_kv, tabs, *, tm, tab_blocks):
    n, d = x.shape
    ql = w_dq.shape[1]
    full = lambda a: pl.BlockSpec(a.shape, lambda i: (0,) * a.ndim)
    tab_spec = pl.BlockSpec((tm, LANES), lambda i: (i % tab_blocks, 0))
    return pl.pallas_call(
        _mla_proj_kernel,
        grid=(n // tm,),
        in_specs=[pl.BlockSpec((tm, d), lambda i: (i, 0)), pl.BlockSpec((1, d), lambda i: (0, 0)),
                  full(w_dq), pl.BlockSpec((1, ql), lambda i: (0, 0)), full(w_uq_ext), full(w_uq_rot),
                  full(w_dkv_ext), pl.BlockSpec((1, KV_LORA), lambda i: (0, 0))] + [tab_spec] * 6,
        out_specs=[pl.BlockSpec((MLA_HEADS, tm, Q_HEAD_W), lambda i: (0, i, 0)),
                   pl.BlockSpec((tm, KV_LORA), lambda i: (i, 0)),
                   pl.BlockSpec((tm, KV_LORA), lambda i: (i, 0)),
                   pl.BlockSpec((tm, LANES), lambda i: (i, 0))],
        out_shape=[jax.ShapeDtypeStruct((MLA_HEADS, n, Q_HEAD_W), BF16),
                   jax.ShapeDtypeStruct((n, KV_LORA), F32),
                   jax.ShapeDtypeStruct((n, KV_LORA), BF16),
                   jax.ShapeDtypeStruct((n, LANES), F32)],
        scratch_shapes=[pltpu.VMEM((tm, ql), BF16)],
        compiler_params=_cparams(("parallel",)),
        name="mla_proj",
    )(x, g.reshape(1, d), w_dq, g_q.reshape(1, ql), w_uq_ext, w_uq_rot, w_dkv_ext,
      g_kv.reshape(1, KV_LORA), *tabs)


def _kv_up_kernel(c_ref, wk_ref, wv_ref, k_ref, v_ref):
    c = c_ref[...]
    for w_ref, o_ref in ((wk_ref, k_ref), (wv_ref, v_ref)):
        for p in range(MLA_HEADS // 2):
            r = _dot(c, w_ref[:, 2 * p * LANES:(2 * p + 2) * LANES]).astype(BF16)
            o_ref[2 * p] = r[:, :LANES]
            o_ref[2 * p + 1] = r[:, LANES:]


def kv_up(ckv, w_uk, w_uv, *, tm):
    n, c = ckv.shape
    w = w_uk.shape[1]
    return pl.pallas_call(
        _kv_up_kernel,
        grid=(n // tm,),
        in_specs=[pl.BlockSpec((tm, c), lambda i: (i, 0)),
                  pl.BlockSpec((c, w), lambda i: (0, 0)),
                  pl.BlockSpec((c, w), lambda i: (0, 0))],
        out_specs=[pl.BlockSpec((MLA_HEADS, tm, LANES), lambda i: (0, i, 0))] * 2,
        out_shape=[jax.ShapeDtypeStruct((MLA_HEADS, n, LANES), BF16)] * 2,
        compiler_params=_cparams(("parallel",)),
        name="kv_up",
    )(ckv, w_uk, w_uv)


def _attn_kernel(qi_ref, ki_ref, last_ref, q_ref, kn_ref, kp_ref, v_ref, o_ref, m_ref, l_ref, acc_ref,
                 *, causal, tq, tk, kv_len):
    step_id = pl.program_id(1)
    qi = qi_ref[step_id]
    ki = ki_ref[step_id]

    @pl.when(ki == 0)
    def _():
        m_ref[...] = jnp.full_like(m_ref, NEG_BIG)
        l_ref[...] = jnp.zeros_like(l_ref)
        acc_ref[...] = jnp.zeros_like(acc_ref)

    def run(masked):
        kp = kp_ref[0]
        ones = jnp.ones((tk, LANES), BF16)
        keep = None
        if masked:
            qc = lax.broadcasted_iota(jnp.int32, (tq, tk), 0) // CHUNK
            kc = lax.broadcasted_iota(jnp.int32, (tq, tk), 1) // CHUNK
            keep = kc <= qc
        elif kv_len is not None:
            keep = ki * tk + lax.broadcasted_iota(jnp.int32, (tq, tk), 1) < kv_len

        def head(h):
            k = jnp.concatenate([kn_ref[h, 0], kp], axis=1)
            s = _dot_nt(q_ref[h, 0], k)
            if keep is not None:
                s = jnp.where(keep, s, NEG_BIG)
            m_old = m_ref[h]
            m_new = jnp.maximum(m_old, jnp.max(s, axis=-1, keepdims=True))
            alpha = jnp.exp2(m_old - m_new)
            p = jnp.exp2(s - jnp.tile(m_new, (1, tk // LANES))).astype(BF16)
            pv = _dot(p, jnp.concatenate([v_ref[h, 0], ones], axis=1))
            l_ref[h] = alpha * l_ref[h] + pv[:, V_HEAD_DIM:]
            acc_ref[h] = alpha * acc_ref[h] + pv[:, :V_HEAD_DIM]
            m_ref[h] = m_new

        def head_group(j, c):
            for k in range(HEADS_PER_ITER):
                head(HEADS_PER_ITER * j + k)
            return c

        lax.fori_loop(0, MLA_HEADS // HEADS_PER_ITER, head_group, 0)

    if causal:
        pl.when(ki < qi)(lambda: run(False))
        pl.when(ki == qi)(lambda: run(True))
    else:
        run(False)

    @pl.when(last_ref[step_id] == 1)
    def _():
        for h in range(MLA_HEADS):
            o_ref[0, :, h * V_HEAD_DIM:(h + 1) * V_HEAD_DIM] = (acc_ref[h] / l_ref[h]).astype(o_ref.dtype)


def attention(q, kn, kp, v, *, tq, tk, causal, kv_len=None):
    _, b, sq, _ = q.shape
    sk = kn.shape[2]
    nq, nk = sq // tq, sk // tk
    if causal:
        assert tq == tk and sq == sk and kv_len is None and tq % CHUNK == 0
        pairs = [(qi, ki) for qi in range(nq) for ki in range(qi + 1)]
    else:
        pairs = [(qi, ki) for qi in range(nq) for ki in range(nk)]
    qi_tab = jnp.asarray(np.array([p[0] for p in pairs], np.int32))
    ki_tab = jnp.asarray(np.array([p[1] for p in pairs], np.int32))
    last = [int(i + 1 == len(pairs) or pairs[i + 1][0] != pairs[i][0]) for i in range(len(pairs))]
    last_tab = jnp.asarray(np.array(last, np.int32))
    grid_spec = pltpu.PrefetchScalarGridSpec(
        num_scalar_prefetch=3,
        grid=(b, len(pairs)),
        in_specs=[
            pl.BlockSpec((MLA_HEADS, 1, tq, Q_HEAD_W), lambda bi, s, qt, kt, lt: (0, bi, qt[s], 0)),
            pl.BlockSpec((MLA_HEADS, 1, tk, QK_NOPE_DIM), lambda bi, s, qt, kt, lt: (0, bi, kt[s], 0)),
            pl.BlockSpec((1, tk, LANES), lambda bi, s, qt, kt, lt: (bi, kt[s], 0)),
            pl.BlockSpec((MLA_HEADS, 1, tk, V_HEAD_DIM), lambda bi, s, qt, kt, lt: (0, bi, kt[s], 0)),
        ],
        out_specs=pl.BlockSpec((1, tq, MLA_HEADS * V_HEAD_DIM), lambda bi, s, qt, kt, lt: (bi, qt[s], 0)),
        scratch_shapes=[pltpu.VMEM((MLA_HEADS, tq, LANES), F32), pltpu.VMEM((MLA_HEADS, tq, LANES), F32),
                        pltpu.VMEM((MLA_HEADS, tq, V_HEAD_DIM), F32)],
    )
    return pl.pallas_call(
        partial(_attn_kernel, causal=causal, tq=tq, tk=tk, kv_len=kv_len),
        grid_spec=grid_spec,
        out_shape=jax.ShapeDtypeStruct((b, sq, MLA_HEADS * V_HEAD_DIM), BF16),
        compiler_params=_cparams(("parallel", "arbitrary")),
        name="attention",
    )(qi_tab, ki_tab, last_tab, q, kn, kp, v)


def _proj_res_kernel(x_ref, a_ref, w_ref, o_ref):
    for c, w in _col_chunks(w_ref.shape[1], 512):
        o_ref[:, c:c + w] = x_ref[:, c:c + w] + _dot(a_ref[...], w_ref[:, c:c + w])


def proj_res(x, a, w, *, tm):
    n, d = x.shape
    k = a.shape[1]
    return pl.pallas_call(
        _proj_res_kernel,
        grid=(n // tm,),
        in_specs=[pl.BlockSpec((tm, d), lambda i: (i, 0)),
                  pl.BlockSpec((tm, k), lambda i: (i, 0)),
                  pl.BlockSpec((k, d), lambda i: (0, 0))],
        out_specs=pl.BlockSpec((tm, d), lambda i: (i, 0)),
        out_shape=jax.ShapeDtypeStruct((n, d), F32),
        compiler_params=_cparams(("parallel",)),
        name="proj_res",
    )(x, a, w)


def _route_kernel(x_ref, g_ref, wr_ref, idx_ref, gate_ref):
    h = _rms(x_ref[...], g_ref[...])
    logits =jnp.dot(h, wr_ref[...], preferred_element_type=F32, precision=lax.Precision.HIGHEST)
    lane_i = lax.broadcasted_iota(jnp.int32, logits.shape, 1)
    lane = lane_i.astype(F32)
    logits = jnp.where(lane_i < N_EXPERTS, logits, -jnp.inf)
    m1 = jnp.max(logits, axis=-1, keepdims=True)
    i1 = jnp.min(jnp.where(logits == m1, lane, float(LANES)), axis=-1, keepdims=True)
    rest = jnp.where(lane == i1, -jnp.inf, logits)
    m2 = jnp.max(rest, axis=-1, keepdims=True)
    i2 = jnp.min(jnp.where(rest == m2, lane, float(LANES)), axis=-1, keepdims=True)
    e2 = jnp.exp(m2 - m1)
    g1 = 1.0 / (1.0 + e2)
    g2 = e2 / (1.0 + e2)
    idx_ref[...] = jnp.where(lane_i == 0, i1, jnp.where(lane_i == 1, i2, 0.0)).astype(jnp.int32)
    gate_ref[...] = jnp.where(lane_i == 0, g1, jnp.where(lane_i == 1, g2, 0.0))


def route(x, g, w_router_pad, *, tm):
    n, d = x.shape
    return pl.pallas_call(
        _route_kernel,
        grid=(n // tm,),
        in_specs=[pl.BlockSpec((tm, d), lambda i: (i, 0)),
                  pl.BlockSpec((1, d), lambda i: (0, 0)),
                  pl.BlockSpec((d, LANES), lambda i: (0, 0))],
        out_specs=[pl.BlockSpec((tm, LANES), lambda i: (i, 0)),
                   pl.BlockSpec((tm, LANES), lambda i: (i, 0))],
        out_shape=[jax.ShapeDtypeStruct((n, LANES), jnp.int32),
                   jax.ShapeDtypeStruct((n, LANES), F32)],
        compiler_params=_cparams(("parallel",)),
        name="moe_route",
    )(x, g.reshape(1, d), w_router_pad)


def _row_copy(src_ref, src_row, dst_ref, dst_row, sem):
    return pltpu.make_async_copy(src_ref.at[pl.ds(src_row, 1)], dst_ref.at[pl.ds(dst_row, 1)], sem)


def _dispatch_kernel(dest_ref, x_ref, g_ref, slots_in_ref, xs_ref, h_ref, sem, *, tm):
    del slots_in_ref
    base = pl.program_id(0) * tm
    h_ref[...] = _rms(x_ref[...], g_ref[...])

    def issue(t, c):
        for k in range(TOP_K):
            _row_copy(h_ref, t, xs_ref, dest_ref[TOP_K * (base + t) + k], sem).start()
        return c

    lax.fori_loop(0, tm, issue, 0, unroll=8)

    def drain(t, c):
        for k in range(TOP_K):
            _row_copy(h_ref, 0, xs_ref, 0, sem).wait()
        return c

    lax.fori_loop(0, tm, drain, 0, unroll=8)


def dispatch(dest, x, g, slots, *, tm):
    n, d = x.shape
    n_slots = slots.shape[0]
    grid_spec = pltpu.PrefetchScalarGridSpec(
        num_scalar_prefetch=1,
        grid=(n // tm,),
        in_specs=[pl.BlockSpec((tm, d), lambda i, ds: (i, 0)),
                  pl.BlockSpec((1, d), lambda i, ds: (0, 0)),
                  pl.BlockSpec(memory_space=pl.ANY)],
        out_specs=pl.BlockSpec(memory_space=pl.ANY),
        scratch_shapes=[pltpu.VMEM((tm, d), F32), pltpu.SemaphoreType.DMA],
    )
    return pl.pallas_call(
        partial(_dispatch_kernel, tm=tm),
        grid_spec=grid_spec,
        out_shape=jax.ShapeDtypeStruct((n_slots, d), F32),
        input_output_aliases={3: 0},
        compiler_params=_cparams(("arbitrary",)),
        name="moe_dispatch",
    )(dest, x, g.reshape(1, d), slots)


def _combine_kernel(dest_ref, x_ref, gate_ref, gf_ref, yb_ref, o_ref, y0_ref, y1_ref, sem, *, tm, final):
    base = pl.program_id(0) * tm
    bufs = (y0_ref, y1_ref)

    def issue(t, c):
        for k in range(TOP_K):
            _row_copy(yb_ref, dest_ref[TOP_K * (base + t) + k], bufs[k], t, sem).start()
        return c

    lax.fori_loop(0, tm, issue, 0, unroll=8)

    def drain(t, c):
        for k in range(TOP_K):
            _row_copy(yb_ref, 0, bufs[k], 0, sem).wait()
        return c

    lax.fori_loop(0, tm, drain, 0, unroll=8)
    g0 = gate_ref[:, 0:1]
    g1 = gate_ref[:, 1:2]
    y = x_ref[...] + (g0 * y0_ref[...] + g1 * y1_ref[...])
    o_ref[...] = _rms(y, gf_ref[...]) if final else y


def combine(dest, x, gates, yb, g_final, *, tm, final):
    n, d = x.shape
    grid_spec = pltpu.PrefetchScalarGridSpec(
        num_scalar_prefetch=1,
        grid=(n // tm,),
        in_specs=[pl.BlockSpec((tm, d), lambda i, ds: (i, 0)),
                  pl.BlockSpec((tm, LANES), lambda i, ds: (i, 0)),
                  pl.BlockSpec((1, d), lambda i, ds: (0, 0)),
                  pl.BlockSpec(memory_space=pl.ANY)],
        out_specs=pl.BlockSpec((tm, d), lambda i, ds: (i, 0)),
        scratch_shapes=[pltpu.VMEM((tm, d), F32), pltpu.VMEM((tm, d), F32), pltpu.SemaphoreType.DMA],
    )
    return pl.pallas_call(
        partial(_combine_kernel, tm=tm, final=final),
        grid_spec=grid_spec,
        out_shape=jax.ShapeDtypeStruct((n, d), F32),
        compiler_params=_cparams(("arbitrary",)),
        name="moe_combine",
    )(dest, x, gates, g_final.reshape(1, d), yb)


def moe(streams, g, w_router_pad, w_gate, w_up, w_down, g_final, *, tm_tok, tm_blk, tf, final):
    routed = [route(x, g, w_router_pad, tm=tm_tok) for x in streams]
    flat_e = jnp.concatenate([idx[:, :TOP_K].reshape(-1) for idx, _ in routed])
    n = flat_e.shape[0] // TOP_K
    d = streams[0].shape[1]
    onehot = (flat_e[:, None] == jnp.arange(N_EXPERTS, dtype=jnp.int32)[None, :]).astype(jnp.int32)
    csum = jnp.cumsum(onehot, axis=0)
    counts = csum[-1]
    padded = (counts + tm_blk - 1) // tm_blk * tm_blk
    pad_ends = jnp.cumsum(padded)
    pad_starts = pad_ends - padded
    dest = jnp.sum(onehot * (pad_starts[None, :] + csum - 1), axis=1).astype(jnp.int32)
    n_blocks = -(-(n * TOP_K + N_EXPERTS * (tm_blk - 1)) // tm_blk)
    block_start = jnp.arange(n_blocks, dtype=jnp.int32) * tm_blk
    block_expert = jnp.minimum(jnp.sum(pad_ends[None, :] <= block_start[:, None], axis=1),
                               N_EXPERTS - 1).astype(jnp.int32)
    n_used = (pad_ends[-1] // tm_blk).astype(jnp.int32).reshape(1)
    slots = jnp.zeros((n_blocks * tm_blk, d), F32)
    dests, start = [], 0
    for x in streams:
        dests.append(dest[start:start + TOP_K * x.shape[0]])
        start += TOP_K * x.shape[0]
        slots = dispatch(dests[-1], x, g, slots, tm=min(tm_tok, x.shape[0]))
    yb = glu(slots, g, w_gate, w_up, w_down, block_expert, n_used, tm=tm_blk, tf=tf, dense=False)
    return [combine(dst, x, gates, yb, g_final, tm=min(tm_tok, x.shape[0]), final=final)
            for dst, x, (_, gates) in zip(dests, streams, routed)]


def _cast_kernel(w_ref, o_ref):
    o_ref[...] = w_ref[...].astype(o_ref.dtype)


def cast_group(w, first, count):
    _, r, c = w.shape
    tr = r
    while tr * c > CAST_BLOCK_ELEMS and tr % 32 == 0:
        tr //= 2
    return pl.pallas_call(
        _cast_kernel,
        grid=(count, r // tr),
        in_specs=[pl.BlockSpec((1, tr, c), lambda e, i: (first + e, i, 0))],
        out_specs=pl.BlockSpec((1, tr, c), lambda e, i: (e, i, 0)),
        out_shape=jax.ShapeDtypeStruct((count, r, c), BF16),
        compiler_params=_cparams(("parallel", "parallel")),
        name="cast_bf16",
    )(w)


def _final_norm_kernel(x_ref, g_ref, o_ref):
    o_ref[...] = _rms(x_ref[...], g_ref[...])


def final_norm(x, g, *, tm):
    n, d = x.shape
    return pl.pallas_call(
        _final_norm_kernel,
        grid=(n // tm,),
        in_specs=[pl.BlockSpec((tm, d), lambda i: (i, 0)), pl.BlockSpec((1, d), lambda i: (0, 0))],
        out_specs=pl.BlockSpec((tm, d), lambda i: (i, 0)),
        out_shape=jax.ShapeDtypeStruct((n, d), F32),
        compiler_params=_cparams(("parallel",)),
        name="final_norm",
    )(x, g.reshape(1, d))


def _rope_tables(pos):
    half = QK_ROPE_DIM // 2
    inv_freq = ROPE_THETA ** (-jnp.arange(half, dtype=F32) / half)
    ang = pos.astype(F32)[:, None] * inv_freq[None, :]
    cos = jnp.tile(jnp.cos(ang), (1, 2))
    sin = jnp.tile(jnp.sin(ang), (1, 2))
    z = jnp.zeros_like(cos)
    cat = lambda a, b: jnp.concatenate([a, b], axis=1)
    return (cat(cos, z), cat(z, cos), cat(sin, z), cat(z, sin), cat(cos, cos), cat(sin, sin))


def _rot_cols(w):
    half = QK_ROPE_DIM // 2
    return jnp.concatenate([-w[..., half:], w[..., :half]], axis=-1)


def _mla_weights(w_uq, w_dkv):
    ql = w_uq.shape[0]
    wq = w_uq.reshape(ql, MLA_HEADS, QK_NOPE_DIM + QK_ROPE_DIM)
    nope, pe = wq[..., :QK_NOPE_DIM], wq[..., QK_NOPE_DIM:]
    z = jnp.zeros_like(pe)
    even = jnp.concatenate([nope, pe, z], axis=-1)
    odd = jnp.concatenate([nope, z, pe], axis=-1)
    is_even = (jnp.arange(MLA_HEADS) % 2 == 0)[None, :, None]
    w_uq_ext = jnp.where(is_even, even, odd).reshape(ql, MLA_HEADS * Q_HEAD_W).astype(BF16)
    w_uq_rot = _rot_cols(pe).reshape(ql, MLA_HEADS * QK_ROPE_DIM).astype(BF16)
    wc, wp = w_dkv[:, :KV_LORA], w_dkv[:, KV_LORA:]
    wr = _rot_cols(wp)
    w_dkv_ext = jnp.concatenate([wc, wp, wp, wr, wr], axis=1).astype(BF16)
    return w_uq_ext, w_uq_rot, w_dkv_ext


def kernel(x_prompt, x_sample, cache_l1_ckv, cache_l1_kpe, cache_l3_ckv, cache_l3_kpe, cache_mem_k, cache_mem_v, mem_prompt, g_mix, g_xattn, g_mem, g_ffn, g_final, sgu_w_in, sgu_b_in, sgu_ln_g, sgu_ln_b, sgu_w_s, sgu_b_s, sgu_w_out, mla_w_dq, mla_g_q, mla_w_uq, mla_w_dkv, mla_g_kv, mla_w_uk, mla_w_uv, mla_w_o, xa_w_q, xa_w_k, xa_w_v, xa_w_o, ffn_w_gate, ffn_w_up, ffn_w_down, moe_w_router, moe_w_gate, moe_w_up, moe_w_down):
    bp, sp, d = x_prompt.shape
    bs, ss, _ = x_sample.shape
    past = cache_l1_ckv.shape[1]
    depth = g_mix.shape[0]
    n_p, n_s = bp * sp, bs * ss
    mem_tokens = mem_prompt.shape[1]
    xa_w = XA_HEADS * XA_HEAD_DIM
    mla_cache = ((cache_l1_ckv, cache_l1_kpe), (cache_l3_ckv, cache_l3_kpe))

    bf = lambda a: a.astype(BF16)
    xp = x_prompt.reshape(n_p, d)
    xs = x_sample.reshape(n_s, d)
    tm_p, tm_s = min(ROW_TILE, n_p), min(ROW_TILE, n_s)
    tq_p = min(ROW_TILE, sp)
    tf = min(FF_TILE, ffn_w_gate.shape[2])

    mk_all, mv_all = mem_kv(mem_prompt.reshape(bp * mem_tokens, d), g_mem, bf(xa_w_k), bf(xa_w_v),
                            tm=min(ROW_TILE, bp * mem_tokens))
    new_mem_k = mk_all.reshape(depth, bp, mem_tokens, XA_HEADS, XA_HEAD_DIM)
    new_mem_v = mv_all.reshape(depth, bp, mem_tokens, XA_HEADS, XA_HEAD_DIM)

    tabs_p = _rope_tables(jnp.arange(sp, dtype=jnp.int32))
    tabs_s = _rope_tables(past + jnp.arange(ss, dtype=jnp.int32))

    ckv_p, kpe_p, ckv_s, kpe_s, sgu_v_s = [], [], [], [], []
    for layer in range(depth):
        i = layer // 2
        if layer % 2 == 0:
            w_in, w_out = bf(sgu_w_in[i]), bf(sgu_w_out[i])
            sgu_args = (w_in, sgu_b_in[i], sgu_ln_g[i], sgu_ln_b[i])
            zp, _ = sgu_in(xp, g_mix[layer], *sgu_args, tm=tm_p, emit_v=False)
            zs, v_new = sgu_in(xs, g_mix[layer], *sgu_args, tm=tm_s, emit_v=True)
            sgu_v_s.append(v_new.reshape(bs, ss, -1))
            b_col = sgu_b_s[i][:, :, None]
            xp = sgu_mix(xp, zp, sgu_w_s[i], b_col, w_out, tm=MIX_TILE, seg=SGU_CHUNK)
            rep = SGU_CHUNK // ss
            w_s_s = jnp.tile(sgu_w_s[i][:, :ss, :ss], (1, rep, rep))
            b_s_s = jnp.tile(sgu_b_s[i][:, :ss], (1, rep))[:, :, None]
            xs = sgu_mix(xs, zs, w_s_s, b_s_s, w_out, tm=MIX_TILE, seg=ss)
        else:
            w_uq_ext, w_uq_rot, w_dkv_ext = _mla_weights(mla_w_uq[i], mla_w_dkv[i])
            proj_w = (bf(mla_w_dq[i]), mla_g_q[i], w_uq_ext, w_uq_rot, w_dkv_ext, mla_g_kv[i])
            w_uk, w_uv, w_o = bf(mla_w_uk[i]), bf(mla_w_uv[i]), bf(mla_w_o[i])
            q, ckv, ckv_b, kpe = mla_proj(xp, g_mix[layer], *proj_w, tabs_p, tm=tq_p, tab_blocks=sp // tq_p)
            ckv_p.append(ckv.reshape(bp, sp, KV_LORA))
            kpe_p.append(kpe[:, :QK_ROPE_DIM].reshape(bp, sp, QK_ROPE_DIM))
            kn, vv = kv_up(ckv_b, w_uk, w_uv, tm=tm_p)
            hm = lambda a, b, s: a.reshape(MLA_HEADS, b, s, a.shape[-1])
            o = attention(hm(q, bp, sp), hm(kn, bp, sp), bf(kpe).reshape(bp, sp, LANES), hm(vv, bp, sp),
                          tq=tq_p, tk=tq_p, causal=True)
            xp = proj_res(xp, o.reshape(n_p, -1), w_o, tm=tm_p)
            q, ckv, ckv_b, kpe = mla_proj(xs, g_mix[layer], *proj_w, tabs_s, tm=ss, tab_blocks=1)
            ckv_s.append(ckv.reshape(bs, ss, KV_LORA))
            kpe_s.append(kpe[:, :QK_ROPE_DIM].reshape(bs, ss, QK_ROPE_DIM))
            c_ckv, c_kpe = mla_cache[i]
            kv_len = past + ss
            sk = -(-kv_len // LANES) * LANES
            pad = sk - kv_len
            ckv_all = jnp.concatenate([bf(c_ckv), ckv_b.reshape(bs, ss, KV_LORA),
                                       jnp.zeros((bs, pad, KV_LORA), BF16)], axis=1)
            kpe_all = jnp.concatenate([bf(jnp.tile(c_kpe, (1, 1, 2))), bf(kpe).reshape(bs, ss, LANES),
                                       jnp.zeros((bs, pad, LANES), BF16)], axis=1)
            kn, vv = kv_up(ckv_all.reshape(bs * sk, KV_LORA), w_uk, w_uv, tm=sk // 4)
            o = attention(hm(q, bs, ss), hm(kn, bs, sk), kpe_all, hm(vv, bs, sk),
                          tq=ss, tk=sk, causal=False, kv_len=kv_len)
            xs = proj_res(xs, o.reshape(n_s, -1), w_o, tm=tm_s)

        w_q, w_o = bf(xa_w_q[layer]), bf(xa_w_o[layer])
        xp = xattn(xp.reshape(bp, sp, d), g_xattn[layer], w_q, bf(mk_all[layer]).reshape(bp, mem_tokens, xa_w),
                   bf(mv_all[layer]).reshape(bp, mem_tokens, xa_w), w_o, tm=tq_p).reshape(n_p, d)
        xs = xattn(xs.reshape(bs, ss, d), g_xattn[layer], w_q,
                   bf(cache_mem_k[layer]).reshape(bs, mem_tokens, xa_w),
                   bf(cache_mem_v[layer]).reshape(bs, mem_tokens, xa_w), w_o, tm=ss).reshape(n_s, d)

        if layer % 2 == 0:
            wg, wu, wd = (cast_group(w, i, 1) for w in (ffn_w_gate, ffn_w_up, ffn_w_down))
            ffn = lambda x, tm: glu(x, g_ffn[layer], wg, wu, wd, jnp.zeros((x.shape[0] // tm,), jnp.int32),
                                    jnp.full((1,), x.shape[0] // tm, jnp.int32), tm=tm, tf=tf, dense=True)
            xp, xs = ffn(xp, tm_p), ffn(xs, tm_s)
        else:
            wr = jnp.pad(moe_w_router[i], ((0, 0), (0, LANES - N_EXPERTS)))
            wg, wu, wd = (cast_group(w.reshape((-1,) + w.shape[2:]), i * N_EXPERTS, N_EXPERTS)
                          for w in (moe_w_gate, moe_w_up, moe_w_down))
            xp, xs = moe([xp, xs], g_ffn[layer], wr, wg, wu, wd, g_final, tm_tok=MIX_TILE, tm_blk=tm_p, tf=tf,
                         final=layer == depth - 1)

    if depth % 2 == 1:
        xp, xs = final_norm(xp, g_final, tm=tm_p), final_norm(xs, g_final, tm=tm_s)
    y_prompt = xp.reshape(bp, sp, d)
    y_sample = xs.reshape(bs, ss, d)
    return (y_prompt, y_sample,
            ckv_p[0], kpe_p[0], ckv_p[1], kpe_p[1],
            new_mem_k, new_mem_v,
            ckv_s[0], kpe_s[0], ckv_s[1], kpe_s[1],
            sgu_v_s[0], sgu_v_s[1])
```

```python
from functools import partial

import numpy as np
import jax
import jax.numpy as jnp
from jax import lax
from jax.experimental import pallas as pl
from jax.experimental.pallas import tpu as pltpu

F32 = jnp.float32
BF16 = jnp.bfloat16

CHUNK = 64
SGU_CHUNK = 128
SGU_GROUPS = 8
MLA_HEADS = 16
QK_NOPE_DIM = 128
QK_ROPE_DIM = 64
V_HEAD_DIM = 128
KV_LORA = 512
ROPE_THETA = 10000.0
XA_HEADS = 4
XA_HEAD_DIM = 128
N_EXPERTS = 8
TOP_K = 2
NORM_EPS = 1e-6

LANES = 128
Q_HEAD_W = 2 * LANES
VMEM_LIMIT = 56 * 1024 * 1024
NEG_BIG = -1e30
ROW_TILE = 512
MIX_TILE = 256
FF_TILE = 1024
CAST_BLOCK_ELEMS = 1024 * 1024
HEADS_PER_ITER = 16


def _cparams(sem):
    return pltpu.CompilerParams(dimension_semantics=sem, vmem_limit_bytes=VMEM_LIMIT)


def _rms(x, g):
    return x * lax.rsqrt(jnp.mean(x * x, axis=-1, keepdims=True) + NORM_EPS) * g


def _dot(a, b):
    return jnp.dot(a, b, preferred_element_type=F32)


def _dot_nt(a, b):
    return lax.dot_general(a, b, (((1,), (1,)), ((), ())), preferred_element_type=F32)


def _col_chunks(n, width):
    return [(c, min(width, n - c)) for c in range(0, n, width)]


def _sgu_in_kernel(x_ref, g_ref, w_ref, b_ref, lng_ref, lnb_ref, z_ref, *rest, emit_v):
    if emit_v:
        v_ref, zs_ref = rest
    else:
        (zs_ref,) = rest
    j = pl.program_id(0)
    h = _rms(x_ref[...], g_ref[...]).astype(BF16)
    for c, w in _col_chunks(w_ref.shape[1], 512):
        a = _dot(h, w_ref[:, c:c + w]) + b_ref[:, c:c + w]
        zs_ref[:, c:c + w] = 0.5 * a * (1.0 + lax.erf(a * np.float32(np.sqrt(0.5))))

    @pl.when(j == 0)
    def _():
        z_ref[...] = zs_ref[...].astype(z_ref.dtype)

    @pl.when(j == 1)
    def _():
        z = zs_ref[...]
        mu = jnp.mean(z, axis=-1, keepdims=True)
        zc = z - mu
        var = jnp.mean(zc * zc, axis=-1, keepdims=True)
        v = zc * lax.rsqrt(var + NORM_EPS) * lng_ref[...] + lnb_ref[...]
        z_ref[...] = v.astype(z_ref.dtype)
        if emit_v:
            v_ref[...] = v


def sgu_in(x, g, w_in, b_in, ln_g, ln_b, *, tm, emit_v):
    n, d = x.shape
    ds = w_in.shape[1] // 2
    nt = n // tm
    out_shape = [jax.ShapeDtypeStruct((n, 2 * ds), BF16)]
    out_specs = [pl.BlockSpec((tm, ds), lambda j, i: (i, j))]
    if emit_v:
        out_shape.append(jax.ShapeDtypeStruct((n, ds), F32))
        out_specs.append(pl.BlockSpec((tm, ds), lambda j, i: (i * j, 0)))
    res = pl.pallas_call(
        partial(_sgu_in_kernel, emit_v=emit_v),
        grid=(2, nt),
        in_specs=[
            pl.BlockSpec((tm, d), lambda j, i: (i, 0)),
            pl.BlockSpec((1, d), lambda j, i: (0, 0)),
            pl.BlockSpec((d, ds), lambda j, i: (0, j)),
            pl.BlockSpec((1, ds), lambda j, i: (0, j)),
            pl.BlockSpec((1, ds), lambda j, i: (0, 0)),
            pl.BlockSpec((1, ds), lambda j, i: (0, 0)),
        ],
        out_specs=out_specs,
        out_shape=out_shape,
        scratch_shapes=[pltpu.VMEM((tm, ds), F32)],
        compiler_params=_cparams(("arbitrary", "arbitrary")),
        name="sgu_in",
    )(x, g.reshape(1, d), w_in, b_in.reshape(1, -1), ln_g.reshape(1, ds), ln_b.reshape(1, ds))
    return res if emit_v else (res[0], None)


def _sgu_mix_kernel(x_ref, u_ref, v_ref, ws_ref, bs_ref, wo_ref, o_ref, gs_ref, *, seg):
    tm = x_ref.shape[0]
    gd = v_ref.shape[1] // SGU_GROUPS
    row = lax.broadcasted_iota(jnp.int32, (SGU_CHUNK, SGU_CHUNK), 0)
    col = lax.broadcasted_iota(jnp.int32, (SGU_CHUNK, SGU_CHUNK), 1)
    keep = (col <= row) & ((col // seg) == (row // seg))
    for g in range(SGU_GROUPS):
        wg = jnp.where(keep, ws_ref[g], 0.0).astype(BF16)
        bg = bs_ref[g]
        for c in range(tm // SGU_CHUNK):
            rs = slice(c * SGU_CHUNK, (c + 1) * SGU_CHUNK)
            cs = slice(g * gd, (g + 1) * gd)
            mixed = _dot(wg, v_ref[rs, cs]) + bg
            gs_ref[rs, cs] = (u_ref[rs, cs].astype(F32) * mixed).astype(BF16)
    for c, w in _col_chunks(wo_ref.shape[1], 512):
        o_ref[:, c:c + w] = x_ref[:, c:c + w] + _dot(gs_ref[...], wo_ref[:, c:c + w])


def sgu_mix(x, z, w_s, b_s, w_out, *, tm, seg):
    n, d = x.shape
    ds = z.shape[1] // 2
    return pl.pallas_call(
        partial(_sgu_mix_kernel, seg=seg),
        grid=(n // tm,),
        in_specs=[
            pl.BlockSpec((tm, d), lambda i: (i, 0)),
            pl.BlockSpec((tm, ds), lambda i: (i, 0)),
            pl.BlockSpec((tm, ds), lambda i: (i, 1)),
            pl.BlockSpec(w_s.shape, lambda i: (0, 0, 0)),
            pl.BlockSpec(b_s.shape, lambda i: (0, 0, 0)),
            pl.BlockSpec(w_out.shape, lambda i: (0, 0)),
        ],
        out_specs=pl.BlockSpec((tm, d), lambda i: (i, 0)),
        out_shape=jax.ShapeDtypeStruct((n, d), F32),
        scratch_shapes=[pltpu.VMEM((tm, ds), BF16)],
        compiler_params=_cparams(("parallel",)),
        name="sgu_mix",
    )(x, z, z, w_s, b_s, w_out)


def _mem_kv_kernel(m_ref, g_ref, wk_ref, wv_ref, k_ref, v_ref):
    h = _rms(m_ref[...], g_ref[0]).astype(BF16)
    k_ref[0] = _dot(h, wk_ref[0])
    v_ref[0] = _dot(h, wv_ref[0])


def mem_kv(mem, g_mem, w_k, w_v, *, tm):
    r, d = mem.shape
    nl, _, w = w_k.shape
    return pl.pallas_call(
        _mem_kv_kernel,
        grid=(nl, r // tm),
        in_specs=[
            pl.BlockSpec((tm, d), lambda l, i: (i, 0)),
            pl.BlockSpec((1, 1, d), lambda l, i: (l, 0, 0)),
            pl.BlockSpec((1, d, w), lambda l, i: (l, 0, 0)),
            pl.BlockSpec((1, d, w), lambda l, i: (l, 0, 0)),
        ],
        out_specs=[pl.BlockSpec((1, tm, w), lambda l, i: (l, i, 0))] * 2,
        out_shape=[jax.ShapeDtypeStruct((nl, r, w), F32)] * 2,
        compiler_params=_cparams(("parallel", "parallel")),
        name="mem_kv",
    )(mem, g_mem.reshape(nl, 1, d), w_k, w_v)


def _top2_route(h, w_hl):
    h_hi = h.astype(BF16)
    h_lo = (h - h_hi.astype(F32)).astype(BF16)
    a = _dot(h_hi, w_hl[...])
    logits = a[:, :LANES] + a[:, LANES:] + _dot(h_lo, w_hl[:, :LANES])
    lane_i = lax.broadcasted_iota(jnp.int32, logits.shape, 1)
    lane = lane_i.astype(F32)
    logits = jnp.where(lane_i < N_EXPERTS, logits, -jnp.inf)
    m1 = jnp.max(logits, axis=-1, keepdims=True)
    i1 = jnp.min(jnp.where(logits == m1, lane, float(LANES)), axis=-1, keepdims=True)
    rest = jnp.where(lane == i1, -jnp.inf, logits)
    m2 = jnp.max(rest, axis=-1, keepdims=True)
    i2 = jnp.min(jnp.where(rest == m2, lane, float(LANES)), axis=-1, keepdims=True)
    e2 = jnp.exp(m2 - m1)
    g1 = 1.0 / (1.0 + e2)
    g2 = e2 / (1.0 + e2)
    idx = jnp.where(lane_i == 0, i1, jnp.where(lane_i == 1, i2, 0.0)).astype(jnp.int32)
    gates = jnp.where(lane_i == 0, g1, jnp.where(lane_i == 1, g2, 0.0))
    return idx, gates


def _xattn_kernel(x_ref, g_ref, wq_ref, k_ref, v_ref, wo_ref, *rest, route):
    if route:
        gf_ref, wr_ref, o_ref, idx_ref, gate_ref, os_ref = rest
    else:
        o_ref, os_ref = rest
    x = x_ref[0]
    h = _rms(x, g_ref[...]).astype(BF16)
    q = (_dot(h, wq_ref[...]) * np.float32(XA_HEAD_DIM ** -0.5)).astype(BF16)
    for hd in range(XA_HEADS):
        cs = slice(hd * XA_HEAD_DIM, (hd + 1) * XA_HEAD_DIM)
        s = _dot_nt(q[:, cs], k_ref[0, :, cs])
        p = jnp.exp(s - jnp.max(s, axis=-1, keepdims=True))
        l = jnp.sum(p, axis=-1, keepdims=True)
        os_ref[:, cs] = (_dot(p.astype(BF16), v_ref[0, :, cs]) / l).astype(BF16)
    for c, w in _col_chunks(wo_ref.shape[1], 512):
        o_ref[0, :, c:c + w] = x[:, c:c + w] + _dot(os_ref[...], wo_ref[:, c:c + w])
    if route:
        idx_ref[0], gate_ref[0] = _top2_route(_rms(o_ref[0], gf_ref[...]), wr_ref)


def xattn(x, g, w_q, mem_k, mem_v, w_o, router=None, *, tm):
    b, s, d = x.shape
    m, xw = mem_k.shape[1:]
    row_spec = lambda w: pl.BlockSpec((1, tm, w), lambda bi, i: (bi, i, 0))
    in_specs = [
        row_spec(d),
        pl.BlockSpec((1, d), lambda bi, i: (0, 0)),
        pl.BlockSpec((d, xw), lambda bi, i: (0, 0)),
        pl.BlockSpec((1, m, xw), lambda bi, i: (bi, 0, 0)),
        pl.BlockSpec((1, m, xw), lambda bi, i: (bi, 0, 0)),
        pl.BlockSpec((xw, d), lambda bi, i: (0, 0)),
    ]
    args = [x, g.reshape(1, d), w_q, mem_k, mem_v, w_o]
    out_specs, out_shape = [row_spec(d)], [jax.ShapeDtypeStruct((b, s, d), F32)]
    if router is not None:
        in_specs += [pl.BlockSpec((1, d), lambda bi, i: (0, 0)), pl.BlockSpec((d, 2 * LANES), lambda bi, i: (0, 0))]
        args += [router[0].reshape(1, d), router[1]]
        out_specs += [row_spec(LANES), row_spec(LANES)]
        out_shape += [jax.ShapeDtypeStruct((b, s, LANES), jnp.int32), jax.ShapeDtypeStruct((b, s, LANES), F32)]
    return pl.pallas_call(
        partial(_xattn_kernel, route=router is not None),
        grid=(b, s // tm),
        in_specs=in_specs,
        out_specs=out_specs,
        out_shape=out_shape,
        scratch_shapes=[pltpu.VMEM((tm, xw), BF16)],
        compiler_params=_cparams(("parallel", "parallel")),
        name="xattn",
    )(*args)


def _glu_kernel(be_ref, nb_ref, x_ref, g_ref, wg_ref, wu_ref, wd_ref, o_ref, h_ref, *, dense):
    i = pl.program_id(0)
    f = pl.program_id(1)

    @pl.when(i < nb_ref[0])
    def _():
        @pl.when(f == 0)
        def _():
            if dense:
                x = x_ref[...]
                h_ref[...] = _rms(x, g_ref[...]).astype(BF16)
                o_ref[...] = x
            else:
                h_ref[...] = x_ref[...].astype(BF16)
                o_ref[...] = jnp.zeros_like(o_ref)

        h = h_ref[...]
        a = _dot(h, wg_ref[0])
        b = _dot(h, wu_ref[0])
        hid = (a * jax.nn.sigmoid(a) * b).astype(BF16)
        o_ref[...] += _dot(hid, wd_ref[0])

    @pl.when(jnp.logical_and(i >= nb_ref[0], f == 0))
    def _():
        o_ref[...] = jnp.zeros_like(o_ref)


def glu(x, g, w_gate, w_up, w_down, block_expert, n_blocks_used, *, tm, tf, dense):
    r, d = x.shape
    ff = w_gate.shape[2]
    nf = ff // tf

    def wsel(i, f, be, nb):
        live = i < nb[0]
        return be[i], jnp.where(live, f, nf - 1)

    grid_spec = pltpu.PrefetchScalarGridSpec(
        num_scalar_prefetch=2,
        grid=(r // tm, nf),
        in_specs=[
            pl.BlockSpec((tm, d), lambda i, f, be, nb: (jnp.minimum(i, nb[0] - 1), 0)),
            pl.BlockSpec((1, d), lambda i, f, be, nb: (0, 0)),
            pl.BlockSpec((1, d, tf), lambda i, f, be, nb: (wsel(i, f, be, nb)[0], 0, wsel(i, f, be, nb)[1])),
            pl.BlockSpec((1, d, tf), lambda i, f, be, nb: (wsel(i, f, be, nb)[0], 0, wsel(i, f, be, nb)[1])),
            pl.BlockSpec((1, tf, d), lambda i, f, be, nb: (wsel(i, f, be, nb)[0], wsel(i, f, be, nb)[1], 0)),
        ],
        out_specs=pl.BlockSpec((tm, d), lambda i, f, be, nb: (i, 0)),
        scratch_shapes=[pltpu.VMEM((tm, d), BF16)],
    )
    return pl.pallas_call(
        partial(_glu_kernel, dense=dense),
        grid_spec=grid_spec,
        out_shape=jax.ShapeDtypeStruct((r, d), F32),
        compiler_params=_cparams(("arbitrary", "arbitrary")),
        name="glu_dense" if dense else "glu_expert",
    )(block_expert, n_blocks_used, x, g.reshape(1, d), w_gate, w_up, w_down)


def _mla_proj_kernel(x_ref, g_ref, wdq_ref, gq_ref, wuq_ref, wrot_ref, wkv_ref, gkv_ref,
                     ce_ref, co_ref, se_ref, so_ref, ck_ref, sk_ref,
                     q_ref, ckv_ref, ckvb_ref, kpe_ref, cq_ref):
    h = _rms(x_ref[...], g_ref[...]).astype(BF16)
    cq_ref[...] = _rms(_dot(h, wdq_ref[...]), gq_ref[...]).astype(BF16)
    kv = _dot(h, wkv_ref[...])
    ckv = _rms(kv[:, :KV_LORA], gkv_ref[...])
    ckv_ref[...] = ckv
    ckvb_ref[...] = ckv.astype(BF16)
    kpe_ref[...] = (kv[:, KV_LORA:KV_LORA + LANES] * ck_ref[...]
                    + kv[:, KV_LORA + LANES:KV_LORA + 2 * LANES] * sk_ref[...])
    scale = np.float32((QK_NOPE_DIM + QK_ROPE_DIM) ** -0.5 * np.log2(np.e))
    cq = cq_ref[...]
    for p in range(MLA_HEADS // 2):
        a = _dot(cq, wuq_ref[:, 2 * p * Q_HEAD_W:(2 * p + 2) * Q_HEAD_W])
        r = _dot(cq, wrot_ref[:, p * LANES:(p + 1) * LANES])
        for k, (c_ref, s_ref) in enumerate(((ce_ref, se_ref), (co_ref, so_ref))):
            q_ref[2 * p + k, :, :LANES] = (a[:, k * Q_HEAD_W:k * Q_HEAD_W + LANES] * scale).astype(BF16)
            pe = a[:, k * Q_HEAD_W + LANES:(k + 1) * Q_HEAD_W] * c_ref[...] + r * s_ref[...]
            q_ref[2 * p + k, :, LANES:] = (pe * scale).astype(BF16)


def mla_proj(x, g, w_dq, g_q, w_uq_ext, w_uq_rot, w_dkv_ext, g_kv, tabs, *, tm, tab_blocks):
    n, d = x.shape
    ql = w_dq.shape[1]
    full = lambda a: pl.BlockSpec(a.shape, lambda i: (0,) * a.ndim)
    tab_spec = pl.BlockSpec((tm, LANES), lambda i: (i % tab_blocks, 0))
    return pl.pallas_call(
        _mla_proj_kernel,
        grid=(n // tm,),
        in_specs=[pl.BlockSpec((tm, d), lambda i: (i, 0)), pl.BlockSpec((1, d), lambda i: (0, 0)),
                  full(w_dq), pl.BlockSpec((1, ql), lambda i: (0, 0)), full(w_uq_ext), full(w_uq_rot),
                  full(w_dkv_ext), pl.BlockSpec((1, KV_LORA), lambda i: (0, 0))] + [tab_spec] * 6,
        out_specs=[pl.BlockSpec((MLA_HEADS, tm, Q_HEAD_W), lambda i: (0, i, 0)),
                   pl.BlockSpec((tm, KV_LORA), lambda i: (i, 0)),
                   pl.BlockSpec((tm, KV_LORA), lambda i: (i, 0)),
                   pl.BlockSpec((tm, LANES), lambda i: (i, 0))],
        out_shape=[jax.ShapeDtypeStruct((MLA_HEADS, n, Q_HEAD_W), BF16),
                   jax.ShapeDtypeStruct((n, KV_LORA), F32),
                   jax.ShapeDtypeStruct((n, KV_LORA), BF16),
                   jax.ShapeDtypeStruct((n, LANES), F32)],
        scratch_shapes=[pltpu.VMEM((tm, ql), BF16)],
        compiler_params=_cparams(("parallel",)),
        name="mla_proj",
    )(x, g.reshape(1, d), w_dq, g_q.reshape(1, ql), w_uq_ext, w_uq_rot, w_dkv_ext,
      g_kv.reshape(1, KV_LORA), *tabs)


def _kv_up_kernel(c_ref, wk_ref, wv_ref, k_ref, v_ref):
    c = c_ref[...]
    for w_ref, o_ref in ((wk_ref, k_ref), (wv_ref, v_ref)):
        for p in range(MLA_HEADS // 2):
            r = _dot(c, w_ref[:, 2 * p * LANES:(2 * p + 2) * LANES]).astype(BF16)
            o_ref[2 * p] = r[:, :LANES]
            o_ref[2 * p + 1] = r[:, LANES:]


def kv_up(ckv, w_uk, w_uv, *, tm):
    n, c = ckv.shape
    w = w_uk.shape[1]
    return pl.pallas_call(
        _kv_up_kernel,
        grid=(n // tm,),
        in_specs=[pl.BlockSpec((tm, c), lambda i: (i, 0)),
                  pl.BlockSpec((c, w), lambda i: (0, 0)),
                  pl.BlockSpec((c, w), lambda i: (0, 0))],
        out_specs=[pl.BlockSpec((MLA_HEADS, tm, LANES), lambda i: (0, i, 0))] * 2,
        out_shape=[jax.ShapeDtypeStruct((MLA_HEADS, n, LANES), BF16)] * 2,
        compiler_params=_cparams(("parallel",)),
        name="kv_up",
    )(ckv, w_uk, w_uv)


def _attn_kernel(qi_ref, ki_ref, last_ref, q_ref, kn_ref, kp_ref, v_ref, o_ref, m_ref, l_ref, acc_ref,
                 *, causal, tq, tk, kv_len):
    step_id = pl.program_id(1)
    qi = qi_ref[step_id]
    ki = ki_ref[step_id]

    @pl.when(ki == 0)
    def _():
        m_ref[...] = jnp.full_like(m_ref, NEG_BIG)
        l_ref[...] = jnp.zeros_like(l_ref)
        acc_ref[...] = jnp.zeros_like(acc_ref)

    def run(masked):
        kp = kp_ref[0]
        ones = jnp.ones((tk, LANES), BF16)
        keep = None
        if masked:
            qc = lax.broadcasted_iota(jnp.int32, (tq, tk), 0) // CHUNK
            kc = lax.broadcasted_iota(jnp.int32, (tq, tk), 1) // CHUNK
            keep = kc <= qc
        elif kv_len is not None:
            keep = ki * tk + lax.broadcasted_iota(jnp.int32, (tq, tk), 1) < kv_len

        def head(h):
            k = jnp.concatenate([kn_ref[h, 0], kp], axis=1)
            s = _dot_nt(q_ref[h, 0], k)
            if keep is not None:
                s = jnp.where(keep, s, NEG_BIG)
            m_old = m_ref[h]
            m_new = jnp.maximum(m_old, jnp.max(s, axis=-1, keepdims=True))
            alpha = jnp.exp2(m_old - m_new)
            p = jnp.exp2(s - jnp.tile(m_new, (1, tk // LANES))).astype(BF16)
            pv = _dot(p, jnp.concatenate([v_ref[h, 0], ones], axis=1))
            l_ref[h] = alpha * l_ref[h] + pv[:, V_HEAD_DIM:]
            acc_ref[h] = alpha * acc_ref[h] + pv[:, :V_HEAD_DIM]
            m_ref[h] = m_new

        def head_group(j, c):
            for k in range(HEADS_PER_ITER):
                head(HEADS_PER_ITER * j + k)
            return c

        lax.fori_loop(0, MLA_HEADS // HEADS_PER_ITER, head_group, 0)

    if causal:
        pl.when(ki < qi)(lambda: run(False))
        pl.when(ki == qi)(lambda: run(True))
    else:
        run(False)

    @pl.when(last_ref[step_id] == 1)
    def _():
        for h in range(MLA_HEADS):
            o_ref[0, :, h * V_HEAD_DIM:(h + 1) * V_HEAD_DIM] = (acc_ref[h] / l_ref[h]).astype(o_ref.dtype)


def attention(q, kn, kp, v, *, tq, tk, causal, kv_len=None):
    _, b, sq, _ = q.shape
    sk = kn.shape[2]
    nq, nk = sq // tq, sk // tk
    if causal:
        assert tq == tk and sq == sk and kv_len is None and tq % CHUNK == 0
        pairs = [(qi, ki) for qi in range(nq) for ki in range(qi + 1)]
    else:
        pairs = [(qi, ki) for qi in range(nq) for ki in range(nk)]
    qi_tab = jnp.asarray(np.array([p[0] for p in pairs], np.int32))
    ki_tab = jnp.asarray(np.array([p[1] for p in pairs], np.int32))
    last = [int(i + 1 == len(pairs) or pairs[i + 1][0] != pairs[i][0]) for i in range(len(pairs))]
    last_tab = jnp.asarray(np.array(last, np.int32))
    grid_spec = pltpu.PrefetchScalarGridSpec(
        num_scalar_prefetch=3,
        grid=(b, len(pairs)),
        in_specs=[
            pl.BlockSpec((MLA_HEADS, 1, tq, Q_HEAD_W), lambda bi, s, qt, kt, lt: (0, bi, qt[s], 0)),
            pl.BlockSpec((MLA_HEADS, 1, tk, QK_NOPE_DIM), lambda bi, s, qt, kt, lt: (0, bi, kt[s], 0)),
            pl.BlockSpec((1, tk, LANES), lambda bi, s, qt, kt, lt: (bi, kt[s], 0)),
            pl.BlockSpec((MLA_HEADS, 1, tk, V_HEAD_DIM), lambda bi, s, qt, kt, lt: (0, bi, kt[s], 0)),
        ],
        out_specs=pl.BlockSpec((1, tq, MLA_HEADS * V_HEAD_DIM), lambda bi, s, qt, kt, lt: (bi, qt[s], 0)),
        scratch_shapes=[pltpu.VMEM((MLA_HEADS, tq, LANES), F32), pltpu.VMEM((MLA_HEADS, tq, LANES), F32),
                        pltpu.VMEM((MLA_HEADS, tq, V_HEAD_DIM), F32)],
    )
    return pl.pallas_call(
        partial(_attn_kernel, causal=causal, tq=tq, tk=tk, kv_len=kv_len),
        grid_spec=grid_spec,
        out_shape=jax.ShapeDtypeStruct((b, sq, MLA_HEADS * V_HEAD_DIM), BF16),
        compiler_params=_cparams(("parallel", "arbitrary")),
        name="attention",
    )(qi_tab, ki_tab, last_tab, q, kn, kp, v)


def _proj_res_kernel(x_ref, a_ref, w_ref, o_ref):
    for c, w in _col_chunks(w_ref.shape[1], 512):
        o_ref[:, c:c + w] = x_ref[:, c:c + w] + _dot(a_ref[...], w_ref[:, c:c + w])


def proj_res(x, a, w, *, tm):
    n, d = x.shape
    k = a.shape[1]
    return pl.pallas_call(
        _proj_res_kernel,
        grid=(n // tm,),
        in_specs=[pl.BlockSpec((tm, d), lambda i: (i, 0)),
                  pl.BlockSpec((tm, k), lambda i: (i, 0)),
                  pl.BlockSpec((k, d), lambda i: (0, 0))],
        out_specs=pl.BlockSpec((tm, d), lambda i: (i, 0)),
        out_shape=jax.ShapeDtypeStruct((n, d), F32),
        compiler_params=_cparams(("parallel",)),
        name="proj_res",
    )(x, a, w)


def _row_copy(src_ref, src_row, dst_ref, dst_row, sem):
    return pltpu.make_async_copy(src_ref.at[pl.ds(src_row, 1)], dst_ref.at[pl.ds(dst_row, 1)], sem)


def _dispatch_kernel(dest_ref, pad_ref, *rest, tm, tiles):
    x_refs = rest[:len(tiles)]
    g_ref, xs_ref, h_ref, z_ref, sem, zsem = rest[len(tiles):]
    i = pl.program_id(0)
    base = i * tm

    @pl.when(i == 0)
    def _():
        z_ref[...] = jnp.zeros_like(z_ref)
        for r in range(N_EXPERTS + 1):
            lo, hi = pad_ref[2 * r], pad_ref[2 * r + 1]

            def zissue(row, c):
                _row_copy(z_ref, 0, xs_ref, row, zsem).start()
                return c

            def zdrain(row, c):
                _row_copy(z_ref, 0, xs_ref, 0, zsem).wait()
                return c

            lax.fori_loop(lo, hi, zissue, 0)
            lax.fori_loop(lo, hi, zdrain, 0)

    first = 0
    for x_ref, n_tiles in zip(x_refs, tiles):
        @pl.when(jnp.logical_and(i >= first, i < first + n_tiles))
        def _(x_ref=x_ref):
            h_ref[...] = _rms(x_ref[...], g_ref[...])
        first += n_tiles

    def issue(t, c):
        for k in range(TOP_K):
            _row_copy(h_ref, t, xs_ref, dest_ref[TOP_K * (base + t) + k], sem).start()
        return c

    lax.fori_loop(0, tm, issue, 0, unroll=8)

    def drain(t, c):
        for k in range(TOP_K):
            _row_copy(h_ref, 0, xs_ref, 0, sem).wait()
        return c

    lax.fori_loop(0, tm, drain, 0, unroll=8)


def dispatch(dest, pad_rows, streams, g, n_slots, *, tm):
    d = streams[0].shape[1]
    tiles = [x.shape[0] // tm for x in streams]
    in_specs, first = [], 0
    for n_tiles in tiles:
        in_specs.append(pl.BlockSpec(
            (tm, d), lambda i, ds, pr, first=first, n_tiles=n_tiles: (jnp.clip(i - first, 0, n_tiles - 1), 0)))
        first += n_tiles
    in_specs.append(pl.BlockSpec((1, d), lambda i, ds, pr: (0, 0)))
    grid_spec = pltpu.PrefetchScalarGridSpec(
        num_scalar_prefetch=2,
        grid=(sum(tiles),),
        in_specs=in_specs,
        out_specs=pl.BlockSpec(memory_space=pl.ANY),
        scratch_shapes=[pltpu.VMEM((tm, d), F32), pltpu.VMEM((8, d), F32),
                        pltpu.SemaphoreType.DMA, pltpu.SemaphoreType.DMA],
    )
    return pl.pallas_call(
        partial(_dispatch_kernel, tm=tm, tiles=tuple(tiles)),
        grid_spec=grid_spec,
        out_shape=jax.ShapeDtypeStruct((n_slots, d), F32),
        compiler_params=_cparams(("arbitrary",)),
        name="moe_dispatch",
    )(dest, pad_rows, *streams, g.reshape(1, d))


def _combine_kernel(dest_ref, x_ref, gate_ref, gf_ref, yb_ref, o_ref, y_ref, sem, *, tm, final):
    i = pl.program_id(0)
    buf = i % 2

    def gather(tile, b):
        def issue(t, c):
            for k in range(TOP_K):
                _row_copy(yb_ref, dest_ref[TOP_K * (tile * tm + t) + k], y_ref.at[b, k], t, sem.at[b]).start()
            return c

        lax.fori_loop(0, tm, issue, 0, unroll=8)

    @pl.when(i == 0)
    def _():
        gather(0, 0)

    @pl.when(i + 1 < pl.num_programs(0))
    def _():
        gather(i + 1, 1 - buf)

    def drain(t, c):
        for k in range(TOP_K):
            _row_copy(yb_ref, 0, y_ref.at[buf, k], 0, sem.at[buf]).wait()
        return c

    lax.fori_loop(0, tm, drain, 0, unroll=8)
    g0 = gate_ref[:, 0:1]
    g1 = gate_ref[:, 1:2]
    y = x_ref[...] + (g0 * y_ref[buf, 0] + g1 * y_ref[buf, 1])
    o_ref[...] = _rms(y, gf_ref[...]) if final else y


def combine(dest, x, gates, yb, g_final, *, tm, final):
    n, d = x.shape
    grid_spec = pltpu.PrefetchScalarGridSpec(
        num_scalar_prefetch=1,
        grid=(n // tm,),
        in_specs=[pl.BlockSpec((tm, d), lambda i, ds: (i, 0)),
                  pl.BlockSpec((tm, LANES), lambda i, ds: (i, 0)),
                  pl.BlockSpec((1, d), lambda i, ds: (0, 0)),
                  pl.BlockSpec(memory_space=pl.ANY)],
        out_specs=pl.BlockSpec((tm, d), lambda i, ds: (i, 0)),
        scratch_shapes=[pltpu.VMEM((2, TOP_K, tm, d), F32), pltpu.SemaphoreType.DMA((2,))],
    )
    return pl.pallas_call(
        partial(_combine_kernel, tm=tm, final=final),
        grid_spec=grid_spec,
        out_shape=jax.ShapeDtypeStruct((n, d), F32),
        compiler_params=_cparams(("arbitrary",)),
        name="moe_combine",
    )(dest, x, gates, g_final.reshape(1, d), yb)


def moe(streams, routed, g, w_gate, w_up, w_down, g_final, *, tm_tok, tm_blk, tf, final):
    flat_e = jnp.concatenate([idx[:, :TOP_K].reshape(-1) for idx, _ in routed])
    n = flat_e.shape[0] // TOP_K
    d = streams[0].shape[1]
    onehot = (flat_e[:, None] == jnp.arange(N_EXPERTS, dtype=jnp.int32)[None, :]).astype(jnp.int32)
    csum = jnp.cumsum(onehot, axis=0)
    counts = csum[-1]
    padded = (counts + tm_blk - 1) // tm_blk * tm_blk
    pad_ends = jnp.cumsum(padded)
    pad_starts = pad_ends - padded
    dest = jnp.sum(onehot * (pad_starts[None, :] + csum - 1), axis=1).astype(jnp.int32)
    n_blocks = -(-(n * TOP_K + N_EXPERTS * (tm_blk - 1)) // tm_blk)
    block_start = jnp.arange(n_blocks, dtype=jnp.int32) * tm_blk
    block_expert = jnp.minimum(jnp.sum(pad_ends[None, :] <= block_start[:, None], axis=1),
                               N_EXPERTS - 1).astype(jnp.int32)
    n_used = (pad_ends[-1] // tm_blk).astype(jnp.int32).reshape(1)
    n_slots = n_blocks * tm_blk
    zero_lo = jnp.concatenate([pad_starts + counts, pad_ends[-1:]])
    zero_hi = jnp.concatenate([pad_ends, jnp.full((1,), n_slots, pad_ends.dtype)])
    pad_rows = jnp.stack([zero_lo, zero_hi], axis=1).reshape(-1).astype(jnp.int32)
    tm_tok = min([tm_tok] + [x.shape[0] for x in streams])
    slots = dispatch(dest, pad_rows, streams, g, n_slots, tm=tm_tok)
    dests, start = [], 0
    for x in streams:
        dests.append(dest[start:start + TOP_K * x.shape[0]])
        start += TOP_K * x.shape[0]
    yb = glu(slots, g, w_gate, w_up, w_down, block_expert, n_used, tm=tm_blk, tf=tf, dense=False)
    return [combine(dst, x, gates, yb, g_final, tm=min(tm_tok, x.shape[0]), final=final)
            for dst, x, (_, gates) in zip(dests, streams, routed)]


def _cast_kernel(w_ref, o_ref):
    o_ref[...] = w_ref[...].astype(o_ref.dtype)


def cast_group(w, first, count):
    _, r, c = w.shape
    tr = r
    while tr * c > CAST_BLOCK_ELEMS and tr % 32 == 0:
        tr //= 2
    return pl.pallas_call(
        _cast_kernel,
        grid=(count, r // tr),
        in_specs=[pl.BlockSpec((1, tr, c), lambda e, i: (first + e, i, 0))],
        out_specs=pl.BlockSpec((1, tr, c), lambda e, i: (e, i, 0)),
        out_shape=jax.ShapeDtypeStruct((count, r, c), BF16),
        compiler_params=_cparams(("parallel", "parallel")),
        name="cast_bf16",
    )(w)


def _final_norm_kernel(x_ref, g_ref, o_ref):
    o_ref[...] = _rms(x_ref[...], g_ref[...])


def final_norm(x, g, *, tm):
    n, d = x.shape
    return pl.pallas_call(
        _final_norm_kernel,
        grid=(n // tm,),
        in_specs=[pl.BlockSpec((tm, d), lambda i: (i, 0)), pl.BlockSpec((1, d), lambda i: (0, 0))],
        out_specs=pl.BlockSpec((tm, d), lambda i: (i, 0)),
        out_shape=jax.ShapeDtypeStruct((n, d), F32),
        compiler_params=_cparams(("parallel",)),
        name="final_norm",
    )(x, g.reshape(1, d))


def _rope_tables(pos):
    half = QK_ROPE_DIM // 2
    inv_freq = ROPE_THETA ** (-jnp.arange(half, dtype=F32) / half)
    ang = pos.astype(F32)[:, None] * inv_freq[None, :]
    cos = jnp.tile(jnp.cos(ang), (1, 2))
    sin = jnp.tile(jnp.sin(ang), (1, 2))
    z = jnp.zeros_like(cos)
    cat = lambda a, b: jnp.concatenate([a, b], axis=1)
    return (cat(cos, z), cat(z, cos), cat(sin, z), cat(z, sin), cat(cos, cos), cat(sin, sin))


def _rot_cols(w):
    half = QK_ROPE_DIM // 2
    return jnp.concatenate([-w[..., half:], w[..., :half]], axis=-1)


def _router_weights(w_router):
    w = jnp.pad(w_router, ((0, 0), (0, LANES - N_EXPERTS)))
    hi = w.astype(BF16)
    lo = (w - hi.astype(F32)).astype(BF16)
    return jnp.concatenate([hi, lo], axis=1)


def _mla_weights(w_uq, w_dkv):
    ql = w_uq.shape[0]
    wq = w_uq.reshape(ql, MLA_HEADS, QK_NOPE_DIM + QK_ROPE_DIM)
    nope, pe = wq[..., :QK_NOPE_DIM], wq[..., QK_NOPE_DIM:]
    z = jnp.zeros_like(pe)
    even = jnp.concatenate([nope, pe, z], axis=-1)
    odd = jnp.concatenate([nope, z, pe], axis=-1)
    is_even = (jnp.arange(MLA_HEADS) % 2 == 0)[None, :, None]
    w_uq_ext = jnp.where(is_even, even, odd).reshape(ql, MLA_HEADS * Q_HEAD_W).astype(BF16)
    w_uq_rot = _rot_cols(pe).reshape(ql, MLA_HEADS * QK_ROPE_DIM).astype(BF16)
    wc, wp = w_dkv[:, :KV_LORA], w_dkv[:, KV_LORA:]
    wr = _rot_cols(wp)
    w_dkv_ext = jnp.concatenate([wc, wp, wp, wr, wr], axis=1).astype(BF16)
    return w_uq_ext, w_uq_rot, w_dkv_ext


def kernel(x_prompt, x_sample, cache_l1_ckv, cache_l1_kpe, cache_l3_ckv, cache_l3_kpe, cache_mem_k, cache_mem_v, mem_prompt, g_mix, g_xattn, g_mem, g_ffn, g_final, sgu_w_in, sgu_b_in, sgu_ln_g, sgu_ln_b, sgu_w_s, sgu_b_s, sgu_w_out, mla_w_dq, mla_g_q, mla_w_uq, mla_w_dkv, mla_g_kv, mla_w_uk, mla_w_uv, mla_w_o, xa_w_q, xa_w_k, xa_w_v, xa_w_o, ffn_w_gate, ffn_w_up, ffn_w_down, moe_w_router, moe_w_gate, moe_w_up, moe_w_down):
    bp, sp, d = x_prompt.shape
    bs, ss, _ = x_sample.shape
    past = cache_l1_ckv.shape[1]
    depth = g_mix.shape[0]
    n_p, n_s = bp * sp, bs * ss
    mem_tokens = mem_prompt.shape[1]
    xa_w = XA_HEADS * XA_HEAD_DIM
    mla_cache = ((cache_l1_ckv, cache_l1_kpe), (cache_l3_ckv, cache_l3_kpe))

    bf = lambda a: a.astype(BF16)
    xp = x_prompt.reshape(n_p, d)
    xs = x_sample.reshape(n_s, d)
    tm_p, tm_s = min(ROW_TILE, n_p), min(ROW_TILE, n_s)
    tq_p = min(ROW_TILE, sp)
    tf = min(FF_TILE, ffn_w_gate.shape[2])

    mk_all, mv_all = mem_kv(mem_prompt.reshape(bp * mem_tokens, d), g_mem, bf(xa_w_k), bf(xa_w_v),
                            tm=min(ROW_TILE, bp * mem_tokens))
    new_mem_k = mk_all.reshape(depth, bp, mem_tokens, XA_HEADS, XA_HEAD_DIM)
    new_mem_v = mv_all.reshape(depth, bp, mem_tokens, XA_HEADS, XA_HEAD_DIM)

    tabs_p = _rope_tables(jnp.arange(sp, dtype=jnp.int32))
    tabs_s = _rope_tables(past + jnp.arange(ss, dtype=jnp.int32))

    ckv_p, kpe_p, ckv_s, kpe_s, sgu_v_s = [], [], [], [], []
    for layer in range(depth):
        i = layer // 2
        if layer % 2 == 0:
            w_in, w_out = bf(sgu_w_in[i]), bf(sgu_w_out[i])
            sgu_args = (w_in, sgu_b_in[i], sgu_ln_g[i], sgu_ln_b[i])
            zp, _ = sgu_in(xp, g_mix[layer], *sgu_args, tm=tm_p, emit_v=False)
            zs, v_new = sgu_in(xs, g_mix[layer], *sgu_args, tm=tm_s, emit_v=True)
            sgu_v_s.append(v_new.reshape(bs, ss, -1))
            b_col = sgu_b_s[i][:, :, None]
            xp = sgu_mix(xp, zp, sgu_w_s[i], b_col, w_out, tm=MIX_TILE, seg=SGU_CHUNK)
            rep = SGU_CHUNK // ss
            w_s_s = jnp.tile(sgu_w_s[i][:, :ss, :ss], (1, rep, rep))
            b_s_s = jnp.tile(sgu_b_s[i][:, :ss], (1, rep))[:, :, None]
            xs = sgu_mix(xs, zs, w_s_s, b_s_s, w_out, tm=MIX_TILE, seg=ss)
        else:
            w_uq_ext, w_uq_rot, w_dkv_ext = _mla_weights(mla_w_uq[i], mla_w_dkv[i])
            proj_w = (bf(mla_w_dq[i]), mla_g_q[i], w_uq_ext, w_uq_rot, w_dkv_ext, mla_g_kv[i])
            w_uk, w_uv, w_o = bf(mla_w_uk[i]), bf(mla_w_uv[i]), bf(mla_w_o[i])
            q, ckv, ckv_b, kpe = mla_proj(xp, g_mix[layer], *proj_w, tabs_p, tm=tq_p, tab_blocks=sp // tq_p)
            ckv_p.append(ckv.reshape(bp, sp, KV_LORA))
            kpe_p.append(kpe[:, :QK_ROPE_DIM].reshape(bp, sp, QK_ROPE_DIM))
            kn, vv = kv_up(ckv_b, w_uk, w_uv, tm=tm_p)
            hm = lambda a, b, s: a.reshape(MLA_HEADS, b, s, a.shape[-1])
            o = attention(hm(q, bp, sp), hm(kn, bp, sp), bf(kpe).reshape(bp, sp, LANES), hm(vv, bp, sp),
                          tq=tq_p, tk=tq_p, causal=True)
            xp = proj_res(xp, o.reshape(n_p, -1), w_o, tm=tm_p)
            q, ckv, ckv_b, kpe = mla_proj(xs, g_mix[layer], *proj_w, tabs_s, tm=ss, tab_blocks=1)
            ckv_s.append(ckv.reshape(bs, ss, KV_LORA))
            kpe_s.append(kpe[:, :QK_ROPE_DIM].reshape(bs, ss, QK_ROPE_DIM))
            c_ckv, c_kpe = mla_cache[i]
            kv_len = past + ss
            sk = -(-kv_len // LANES) * LANES
            pad = sk - kv_len
            ckv_all = jnp.concatenate([bf(c_ckv), ckv_b.reshape(bs, ss, KV_LORA),
                                       jnp.zeros((bs, pad, KV_LORA), BF16)], axis=1)
            kpe_all = jnp.concatenate([bf(jnp.tile(c_kpe, (1, 1, 2))), bf(kpe).reshape(bs, ss, LANES),
                                       jnp.zeros((bs, pad, LANES), BF16)], axis=1)
            kn, vv = kv_up(ckv_all.reshape(bs * sk, KV_LORA), w_uk, w_uv, tm=sk // 4)
            o = attention(hm(q, bs, ss), hm(kn, bs, sk), kpe_all, hm(vv, bs, sk),
                          tq=ss, tk=sk, causal=False, kv_len=kv_len)
            xs = proj_res(xs, o.reshape(n_s, -1), w_o, tm=tm_s)

        w_q, w_o = bf(xa_w_q[layer]), bf(xa_w_o[layer])
        is_moe = layer % 2 == 1
        router = (g_ffn[layer], _router_weights(moe_w_router[i])) if is_moe else None
        rp = xattn(xp.reshape(bp, sp, d), g_xattn[layer], w_q, bf(mk_all[layer]).reshape(bp, mem_tokens, xa_w),
                   bf(mv_all[layer]).reshape(bp, mem_tokens, xa_w), w_o, router, tm=tq_p)
        rs = xattn(xs.reshape(bs, ss, d), g_xattn[layer], w_q,
                   bf(cache_mem_k[layer]).reshape(bs, mem_tokens, xa_w),
                   bf(cache_mem_v[layer]).reshape(bs, mem_tokens, xa_w), w_o, router, tm=ss)
        xp, xs = rp[0].reshape(n_p, d), rs[0].reshape(n_s, d)

        if not is_moe:
            wg, wu, wd = (cast_group(w, i, 1) for w in (ffn_w_gate, ffn_w_up, ffn_w_down))
            ffn = lambda x, tm: glu(x, g_ffn[layer], wg, wu, wd, jnp.zeros((x.shape[0] // tm,), jnp.int32),
                                    jnp.full((1,), x.shape[0] // tm, jnp.int32), tm=tm, tf=tf, dense=True)
            xp, xs = ffn(xp, tm_p), ffn(xs, tm_s)
        else:
            wg, wu, wd = (cast_group(w.reshape((-1,) + w.shape[2:]), i * N_EXPERTS, N_EXPERTS)
                          for w in (moe_w_gate, moe_w_up, moe_w_down))
            routed = [(r[1].reshape(-1, LANES), r[2].reshape(-1, LANES)) for r in (rp, rs)]
            xp, xs = moe([xp, xs], routed, g_ffn[layer], wg, wu, wd, g_final, tm_tok=MIX_TILE, tm_blk=tm_p, tf=tf,
                         final=layer == depth - 1)

    if depth % 2 == 1:
        xp, xs = final_norm(xp, g_final, tm=tm_p), final_norm(xs, g_final, tm=tm_s)
    y_prompt = xp.reshape(bp, sp, d)
    y_sample = xs.reshape(bs, ss, d)
    return (y_prompt, y_sample,
            ckv_p[0], kpe_p[0], ckv_p[1], kpe_p[1],
            new_mem_k, new_mem_v,
            ckv_s[0], kpe_s[0], ckv_s[1], kpe_s[1],
            sgu_v_s[0], sgu_v_s[1])
```

```python
from functools import partial

import numpy as np
import jax
import jax.numpy as jnp
from jax import lax
from jax.experimental import pallas as pl
from jax.experimental.pallas import tpu as pltpu

F32 = jnp.float32
BF16 = jnp.bfloat16

CHUNK = 64
SGU_CHUNK = 128
SGU_GROUPS = 8
MLA_HEADS = 16
QK_NOPE_DIM = 128
QK_ROPE_DIM = 64
V_HEAD_DIM = 128
KV_LORA = 512
ROPE_THETA = 10000.0
XA_HEADS = 4
XA_HEAD_DIM = 128
N_EXPERTS = 8
TOP_K = 2
NORM_EPS = 1e-6

LANES = 128
Q_HEAD_W = 2 * LANES
VMEM_LIMIT = 56 * 1024 * 1024
NEG_BIG = -1e30
ROW_TILE = 512
MIX_TILE = 256
MOE_ROW_TILE = 512
FF_TILE = 1024
COL_CHUNK = 512
CAST_BLOCK_ELEMS = 1024 * 1024
HEADS_PER_ITER = 16


def _cparams(sem):
    return pltpu.CompilerParams(dimension_semantics=sem, vmem_limit_bytes=VMEM_LIMIT)


def _rms(x, g):
    return x * lax.rsqrt(jnp.mean(x * x, axis=-1, keepdims=True) + NORM_EPS) * g


def _dot(a, b):
    return jnp.dot(a, b, preferred_element_type=F32)


def _dot_nt(a, b):
    return lax.dot_general(a, b, (((1,), (1,)), ((), ())), preferred_element_type=F32)


def _col_chunks(n, width):
    return [(c, min(width, n - c)) for c in range(0, n, width)]


def _sgu_in_kernel(x_ref, g_ref, w_ref, b_ref, lng_ref, lnb_ref, z_ref, *rest, emit_v):
    if emit_v:
        v_ref, zs_ref = rest
    else:
        (zs_ref,) = rest
    j = pl.program_id(0)
    h = _rms(x_ref[...], g_ref[...]).astype(BF16)
    for c, w in _col_chunks(w_ref.shape[1], COL_CHUNK):
        a = _dot(h, w_ref[:, c:c + w]) + b_ref[:, c:c + w]
        zs_ref[:, c:c + w] = 0.5 * a * (1.0 + lax.erf(a * np.float32(np.sqrt(0.5))))

    @pl.when(j == 0)
    def _():
        z_ref[...] = zs_ref[...].astype(z_ref.dtype)

    @pl.when(j == 1)
    def _():
        z = zs_ref[...]
        mu = jnp.mean(z, axis=-1, keepdims=True)
        zc = z - mu
        var = jnp.mean(zc * zc, axis=-1, keepdims=True)
        v = zc * lax.rsqrt(var + NORM_EPS) * lng_ref[...] + lnb_ref[...]
        z_ref[...] = v.astype(z_ref.dtype)
        if emit_v:
            v_ref[...] = v


def sgu_in(x, g, w_in, b_in, ln_g, ln_b, *, tm, emit_v):
    n, d = x.shape
    ds = w_in.shape[1] // 2
    nt = n // tm
    out_shape = [jax.ShapeDtypeStruct((n, 2 * ds), BF16)]
    out_specs = [pl.BlockSpec((tm, ds), lambda j, i: (i, j))]
    if emit_v:
        out_shape.append(jax.ShapeDtypeStruct((n, ds), F32))
        out_specs.append(pl.BlockSpec((tm, ds), lambda j, i: (i * j, 0)))
    res = pl.pallas_call(
        partial(_sgu_in_kernel, emit_v=emit_v),
        grid=(2, nt),
        in_specs=[
            pl.BlockSpec((tm, d), lambda j, i: (i, 0)),
            pl.BlockSpec((1, d), lambda j, i: (0, 0)),
            pl.BlockSpec((d, ds), lambda j, i: (0, j)),
            pl.BlockSpec((1, ds), lambda j, i: (0, j)),
            pl.BlockSpec((1, ds), lambda j, i: (0, 0)),
            pl.BlockSpec((1, ds), lambda j, i: (0, 0)),
        ],
        out_specs=out_specs,
        out_shape=out_shape,
        scratch_shapes=[pltpu.VMEM((tm, ds), F32)],
        compiler_params=_cparams(("arbitrary", "arbitrary")),
        name="sgu_in",
    )(x, g.reshape(1, d), w_in, b_in.reshape(1, -1), ln_g.reshape(1, ds), ln_b.reshape(1, ds))
    return res if emit_v else (res[0], None)


def _sgu_mix_kernel(x_ref, u_ref, v_ref, ws_ref, bs_ref, wo_ref, o_ref, gs_ref, *, seg):
    tm = x_ref.shape[0]
    gd = v_ref.shape[1] // SGU_GROUPS
    row = lax.broadcasted_iota(jnp.int32, (SGU_CHUNK, SGU_CHUNK), 0)
    col = lax.broadcasted_iota(jnp.int32, (SGU_CHUNK, SGU_CHUNK), 1)
    keep = (col <= row) & ((col // seg) == (row // seg))
    for g in range(SGU_GROUPS):
        wg = jnp.where(keep, ws_ref[g], 0.0).astype(BF16)
        bg = bs_ref[g]
        for c in range(tm // SGU_CHUNK):
            rs = slice(c * SGU_CHUNK, (c + 1) * SGU_CHUNK)
            cs = slice(g * gd, (g + 1) * gd)
            mixed = _dot(wg, v_ref[rs, cs]) + bg
            gs_ref[rs, cs] = (u_ref[rs, cs].astype(F32) * mixed).astype(BF16)
    for c, w in _col_chunks(wo_ref.shape[1], COL_CHUNK):
        o_ref[:, c:c + w] = x_ref[:, c:c + w] + _dot(gs_ref[...], wo_ref[:, c:c + w])


def sgu_mix(x, z, w_s, b_s, w_out, *, tm, seg):
    n, d = x.shape
    ds = z.shape[1] // 2
    return pl.pallas_call(
        partial(_sgu_mix_kernel, seg=seg),
        grid=(n // tm,),
        in_specs=[
            pl.BlockSpec((tm, d), lambda i: (i, 0)),
            pl.BlockSpec((tm, ds), lambda i: (i, 0)),
            pl.BlockSpec((tm, ds), lambda i: (i, 1)),
            pl.BlockSpec(w_s.shape, lambda i: (0, 0, 0)),
            pl.BlockSpec(b_s.shape, lambda i: (0, 0, 0)),
            pl.BlockSpec(w_out.shape, lambda i: (0, 0)),
        ],
        out_specs=pl.BlockSpec((tm, d), lambda i: (i, 0)),
        out_shape=jax.ShapeDtypeStruct((n, d), F32),
        scratch_shapes=[pltpu.VMEM((tm, ds), BF16)],
        compiler_params=_cparams(("parallel",)),
        name="sgu_mix",
    )(x, z, z, w_s, b_s, w_out)


def _mem_kv_kernel(m_ref, g_ref, wk_ref, wv_ref, k_ref, v_ref):
    h = _rms(m_ref[...], g_ref[0]).astype(BF16)
    k_ref[0] = _dot(h, wk_ref[0])
    v_ref[0] = _dot(h, wv_ref[0])


def mem_kv(mem, g_mem, w_k, w_v, *, tm):
    r, d = mem.shape
    nl, _, w = w_k.shape
    return pl.pallas_call(
        _mem_kv_kernel,
        grid=(nl, r // tm),
        in_specs=[
            pl.BlockSpec((tm, d), lambda l, i: (i, 0)),
            pl.BlockSpec((1, 1, d), lambda l, i: (l, 0, 0)),
            pl.BlockSpec((1, d, w), lambda l, i: (l, 0, 0)),
            pl.BlockSpec((1, d, w), lambda l, i: (l, 0, 0)),
        ],
        out_specs=[pl.BlockSpec((1, tm, w), lambda l, i: (l, i, 0))] * 2,
        out_shape=[jax.ShapeDtypeStruct((nl, r, w), F32)] * 2,
        compiler_params=_cparams(("parallel", "parallel")),
        name="mem_kv",
    )(mem, g_mem.reshape(nl, 1, d), w_k, w_v)


def _top2_route(h, w_hl):
    h_hi = h.astype(BF16)
    h_lo = (h - h_hi.astype(F32)).astype(BF16)
    a = _dot(h_hi, w_hl[...])
    logits = a[:, :LANES] + a[:, LANES:] + _dot(h_lo, w_hl[:, :LANES])
    lane_i = lax.broadcasted_iota(jnp.int32, logits.shape, 1)
    lane = lane_i.astype(F32)
    logits = jnp.where(lane_i < N_EXPERTS, logits, -jnp.inf)
    m1 = jnp.max(logits, axis=-1, keepdims=True)
    i1 = jnp.min(jnp.where(logits == m1, lane, float(LANES)), axis=-1, keepdims=True)
    rest = jnp.where(lane == i1, -jnp.inf, logits)
    m2 = jnp.max(rest, axis=-1, keepdims=True)
    i2 = jnp.min(jnp.where(rest == m2, lane, float(LANES)), axis=-1, keepdims=True)
    e2 = jnp.exp(m2 - m1)
    g1 = 1.0 / (1.0 + e2)
    g2 = e2 / (1.0 + e2)
    idx = jnp.where(lane_i == 0, i1, jnp.where(lane_i == 1, i2, 0.0)).astype(jnp.int32)
    gates = jnp.where(lane_i == 0, g1, jnp.where(lane_i == 1, g2, 0.0))
    return idx, gates


def _xattn_kernel(x_ref, g_ref, wq_ref, k_ref, v_ref, wo_ref, *rest, route):
    if route:
        gf_ref, wr_ref, o_ref, idx_ref, gate_ref, os_ref = rest
    else:
        o_ref, os_ref = rest
    x = x_ref[0]
    h = _rms(x, g_ref[...]).astype(BF16)
    q = (_dot(h, wq_ref[...]) * np.float32(XA_HEAD_DIM ** -0.5)).astype(BF16)
    for hd in range(XA_HEADS):
        cs = slice(hd * XA_HEAD_DIM, (hd + 1) * XA_HEAD_DIM)
        s = _dot_nt(q[:, cs], k_ref[0, :, cs])
        p = jnp.exp(s - jnp.max(s, axis=-1, keepdims=True))
        l = jnp.sum(p, axis=-1, keepdims=True)
        os_ref[:, cs] = (_dot(p.astype(BF16), v_ref[0, :, cs]) / l).astype(BF16)
    for c, w in _col_chunks(wo_ref.shape[1], COL_CHUNK):
        o_ref[0, :, c:c + w] = x[:, c:c + w] + _dot(os_ref[...], wo_ref[:, c:c + w])
    if route:
        idx_ref[0], gate_ref[0] = _top2_route(_rms(o_ref[0], gf_ref[...]), wr_ref)


def xattn(x, g, w_q, mem_k, mem_v, w_o, router=None, *, tm):
    b, s, d = x.shape
    m, xw = mem_k.shape[1:]
    row_spec = lambda w: pl.BlockSpec((1, tm, w), lambda bi, i: (bi, i, 0))
    in_specs = [
        row_spec(d),
        pl.BlockSpec((1, d), lambda bi, i: (0, 0)),
        pl.BlockSpec((d, xw), lambda bi, i: (0, 0)),
        pl.BlockSpec((1, m, xw), lambda bi, i: (bi, 0, 0)),
        pl.BlockSpec((1, m, xw), lambda bi, i: (bi, 0, 0)),
        pl.BlockSpec((xw, d), lambda bi, i: (0, 0)),
    ]
    args = [x, g.reshape(1, d), w_q, mem_k, mem_v, w_o]
    out_specs, out_shape = [row_spec(d)], [jax.ShapeDtypeStruct((b, s, d), F32)]
    if router is not None:
        in_specs += [pl.BlockSpec((1, d), lambda bi, i: (0, 0)), pl.BlockSpec((d, 2 * LANES), lambda bi, i: (0, 0))]
        args += [router[0].reshape(1, d), router[1]]
        out_specs += [row_spec(LANES), row_spec(LANES)]
        out_shape += [jax.ShapeDtypeStruct((b, s, LANES), jnp.int32), jax.ShapeDtypeStruct((b, s, LANES), F32)]
    return pl.pallas_call(
        partial(_xattn_kernel, route=router is not None),
        grid=(b, s // tm),
        in_specs=in_specs,
        out_specs=out_specs,
        out_shape=out_shape,
        scratch_shapes=[pltpu.VMEM((tm, xw), BF16)],
        compiler_params=_cparams(("parallel", "parallel")),
        name="xattn",
    )(*args)


def _glu_kernel(be_ref, nb_ref, x_ref, g_ref, wg_ref, wu_ref, wd_ref, o_ref, h_ref, *, dense):
    i = pl.program_id(0)
    f = pl.program_id(1)

    @pl.when(i < nb_ref[0])
    def _():
        @pl.when(f == 0)
        def _():
            if dense:
                x = x_ref[...]
                h_ref[...] = _rms(x, g_ref[...]).astype(BF16)
                o_ref[...] = x
            else:
                h_ref[...] = x_ref[...].astype(BF16)
                o_ref[...] = jnp.zeros_like(o_ref)

        h = h_ref[...]
        a = _dot(h, wg_ref[0])
        b = _dot(h, wu_ref[0])
        hid = (a * jax.nn.sigmoid(a) * b).astype(BF16)
        o_ref[...] += _dot(hid, wd_ref[0])

    @pl.when(jnp.logical_and(i >= nb_ref[0], f == 0))
    def _():
        o_ref[...] = jnp.zeros_like(o_ref)


def glu(x, g, w_gate, w_up, w_down, block_expert, n_blocks_used, *, tm, tf, dense):
    r, d = x.shape
    ff = w_gate.shape[2]
    nf = ff // tf

    def wsel(i, f, be, nb):
        live = i < nb[0]
        return be[i], jnp.where(live, f, nf - 1)

    grid_spec = pltpu.PrefetchScalarGridSpec(
        num_scalar_prefetch=2,
        grid=(r // tm, nf),
        in_specs=[
            pl.BlockSpec((tm, d), lambda i, f, be, nb: (jnp.minimum(i, nb[0] - 1), 0)),
            pl.BlockSpec((1, d), lambda i, f, be, nb: (0, 0)),
            pl.BlockSpec((1, d, tf), lambda i, f, be, nb: (wsel(i, f, be, nb)[0], 0, wsel(i, f, be, nb)[1])),
            pl.BlockSpec((1, d, tf), lambda i, f, be, nb: (wsel(i, f, be, nb)[0], 0, wsel(i, f, be, nb)[1])),
            pl.BlockSpec((1, tf, d), lambda i, f, be, nb: (wsel(i, f, be, nb)[0], wsel(i, f, be, nb)[1], 0)),
        ],
        out_specs=pl.BlockSpec((tm, d), lambda i, f, be, nb: (i, 0)),
        scratch_shapes=[pltpu.VMEM((tm, d), BF16)],
    )
    return pl.pallas_call(
        partial(_glu_kernel, dense=dense),
        grid_spec=grid_spec,
        out_shape=jax.ShapeDtypeStruct((r, d), F32),
        compiler_params=_cparams(("arbitrary", "arbitrary")),
        name="glu_dense" if dense else "glu_expert",
    )(block_expert, n_blocks_used, x, g.reshape(1, d), w_gate, w_up, w_down)


def _mla_proj_kernel(x_ref, g_ref, wdq_ref, gq_ref, wuq_ref, wrot_ref, wkv_ref, gkv_ref,
                     ce_ref, co_ref, se_ref, so_ref, ck_ref, sk_ref,
                     q_ref, ckv_ref, ckvb_ref, kpe_ref, cq_ref):
    h = _rms(x_ref[...], g_ref[...]).astype(BF16)
    cq_ref[...] = _rms(_dot(h, wdq_ref[...]), gq_ref[...]).astype(BF16)
    kv = _dot(h, wkv_ref[...])
    ckv = _rms(kv[:, :KV_LORA], gkv_ref[...])
    ckv_ref[...] = ckv
    ckvb_ref[...] = ckv.astype(BF16)
    kpe_ref[...] = (kv[:, KV_LORA:KV_LORA + LANES] * ck_ref[...]
                    + kv[:, KV_LORA + LANES:KV_LORA + 2 * LANES] * sk_ref[...])
    scale = np.float32((QK_NOPE_DIM + QK_ROPE_DIM) ** -0.5 * np.log2(np.e))
    cq = cq_ref[...]
    for p in range(MLA_HEADS // 2):
        a = _dot(cq, wuq_ref[:, 2 * p * Q_HEAD_W:(2 * p + 2) * Q_HEAD_W])
        r = _dot(cq, wrot_ref[:, p * LANES:(p + 1) * LANES])
        for k, (c_ref, s_ref) in enumerate(((ce_ref, se_ref), (co_ref, so_ref))):
            q_ref[2 * p + k, :, :LANES] = (a[:, k * Q_HEAD_W:k * Q_HEAD_W + LANES] * scale).astype(BF16)
            pe = a[:, k * Q_HEAD_W + LANES:(k + 1) * Q_HEAD_W] * c_ref[...] + r * s_ref[...]
            q_ref[2 * p + k, :, LANES:] = (pe * scale).astype(BF16)


def mla_proj(x, g, w_dq, g_q, w_uq_ext, w_uq_rot, w_dkv_ext, g_kv, tabs, *, tm, tab_blocks):
    n, d = x.shape
    ql = w_dq.shape[1]
    full = lambda a: pl.BlockSpec(a.shape, lambda i: (0,) * a.ndim)
    tab_spec = pl.BlockSpec((tm, LANES), lambda i: (i % tab_blocks, 0))
    return pl.pallas_call(
        _mla_proj_kernel,
        grid=(n // tm,),
        in_specs=[pl.BlockSpec((tm, d), lambda i: (i, 0)), pl.BlockSpec((1, d), lambda i: (0, 0)),
                  full(w_dq), pl.BlockSpec((1, ql), lambda i: (0, 0)), full(w_uq_ext), full(w_uq_rot),
                  full(w_dkv_ext), pl.BlockSpec((1, KV_LORA), lambda i: (0, 0))] + [tab_spec] * 6,
        out_specs=[pl.BlockSpec((MLA_HEADS, tm, Q_HEAD_W), lambda i: (0, i, 0)),
                   pl.BlockSpec((tm, KV_LORA), lambda i: (i, 0)),
                   pl.BlockSpec((tm, KV_LORA), lambda i: (i, 0)),
                   pl.BlockSpec((tm, LANES), lambda i: (i, 0))],
        out_shape=[jax.ShapeDtypeStruct((MLA_HEADS, n, Q_HEAD_W), BF16),
                   jax.ShapeDtypeStruct((n, KV_LORA), F32),
                   jax.ShapeDtypeStruct((n, KV_LORA), BF16),
                   jax.ShapeDtypeStruct((n, LANES), F32)],
        scratch_shapes=[pltpu.VMEM((tm, ql), BF16)],
        compiler_params=_cparams(("parallel",)),
        name="mla_proj",
    )(x, g.reshape(1, d), w_dq, g_q.reshape(1, ql), w_uq_ext, w_uq_rot, w_dkv_ext,
      g_kv.reshape(1, KV_LORA), *tabs)


def _kv_up_kernel(c_ref, wk_ref, wv_ref, k_ref, v_ref):
    c = c_ref[...]
    for w_ref, o_ref in ((wk_ref, k_ref), (wv_ref, v_ref)):
        for p in range(MLA_HEADS // 2):
            r = _dot(c, w_ref[:, 2 * p * LANES:(2 * p + 2) * LANES]).astype(BF16)
            o_ref[2 * p] = r[:, :LANES]
            o_ref[2 * p + 1] = r[:, LANES:]


def kv_up(ckv, w_uk, w_uv, *, tm):
    n, c = ckv.shape
    w = w_uk.shape[1]
    return pl.pallas_call(
        _kv_up_kernel,
        grid=(n // tm,),
        in_specs=[pl.BlockSpec((tm, c), lambda i: (i, 0)),
                  pl.BlockSpec((c, w), lambda i: (0, 0)),
                  pl.BlockSpec((c, w), lambda i: (0, 0))],
        out_specs=[pl.BlockSpec((MLA_HEADS, tm, LANES), lambda i: (0, i, 0))] * 2,
        out_shape=[jax.ShapeDtypeStruct((MLA_HEADS, n, LANES), BF16)] * 2,
        compiler_params=_cparams(("parallel",)),
        name="kv_up",
    )(ckv, w_uk, w_uv)


def _attn_kernel(qi_ref, ki_ref, q_ref, kn_ref, kp_ref, v_ref, o_ref, m_ref, l_ref, acc_ref, *, t):
    tq = tk = t
    step_id = pl.program_id(1)
    qi = qi_ref[step_id]
    ki = ki_ref[step_id]

    @pl.when(ki == 0)
    def _():
        m_ref[...] = jnp.full_like(m_ref, NEG_BIG)
        l_ref[...] = jnp.zeros_like(l_ref)
        acc_ref[...] = jnp.zeros_like(acc_ref)

    def run(masked):
        kp = kp_ref[0]
        ones = jnp.ones((tk, LANES), BF16)
        keep = None
        if masked:
            qc = lax.broadcasted_iota(jnp.int32, (tq, tk), 0) // CHUNK
            kc = lax.broadcasted_iota(jnp.int32, (tq, tk), 1) // CHUNK
            keep = kc <= qc

        def head(h):
            k = jnp.concatenate([kn_ref[h, 0], kp], axis=1)
            s = _dot_nt(q_ref[h, 0], k)
            if keep is not None:
                s = jnp.where(keep, s, NEG_BIG)
            m_old = m_ref[h]
            m_new = jnp.maximum(m_old, jnp.max(s, axis=-1, keepdims=True))
            alpha = jnp.exp2(m_old - m_new)
            p = jnp.exp2(s - jnp.tile(m_new, (1, tk // LANES))).astype(BF16)
            pv = _dot(p, jnp.concatenate([v_ref[h, 0], ones], axis=1))
            l_ref[h] = alpha * l_ref[h] + pv[:, V_HEAD_DIM:]
            acc_ref[h] = alpha * acc_ref[h] + pv[:, :V_HEAD_DIM]
            m_ref[h] = m_new

        def head_group(j, c):
            for k in range(HEADS_PER_ITER):
                head(HEADS_PER_ITER * j + k)
            return c

        lax.fori_loop(0, MLA_HEADS // HEADS_PER_ITER, head_group, 0)

    pl.when(ki < qi)(lambda: run(False))
    pl.when(ki == qi)(lambda: run(True))

    @pl.when(ki == qi)
    def _():
        for h in range(MLA_HEADS):
            o_ref[0, :, h * V_HEAD_DIM:(h + 1) * V_HEAD_DIM] = (acc_ref[h] / l_ref[h]).astype(o_ref.dtype)


def attention(q, kn, kp, v, *, t):
    _, b, s, _ = q.shape
    assert s % t == 0 and t % CHUNK == 0
    pairs = [(qi, ki) for qi in range(s // t) for ki in range(qi + 1)]
    qi_tab = jnp.asarray(np.array([p[0] for p in pairs], np.int32))
    ki_tab = jnp.asarray(np.array([p[1] for p in pairs], np.int32))
    grid_spec = pltpu.PrefetchScalarGridSpec(
        num_scalar_prefetch=2,
        grid=(b, len(pairs)),
        in_specs=[
            pl.BlockSpec((MLA_HEADS, 1, t, Q_HEAD_W), lambda bi, s, qt, kt: (0, bi, qt[s], 0)),
            pl.BlockSpec((MLA_HEADS, 1, t, QK_NOPE_DIM), lambda bi, s, qt, kt: (0, bi, kt[s], 0)),
            pl.BlockSpec((1, t, LANES), lambda bi, s, qt, kt: (bi, kt[s], 0)),
            pl.BlockSpec((MLA_HEADS, 1, t, V_HEAD_DIM), lambda bi, s, qt, kt: (0, bi, kt[s], 0)),
        ],
        out_specs=pl.BlockSpec((1, t, MLA_HEADS * V_HEAD_DIM), lambda bi, s, qt, kt: (bi, qt[s], 0)),
        scratch_shapes=[pltpu.VMEM((MLA_HEADS, t, LANES), F32), pltpu.VMEM((MLA_HEADS, t, LANES), F32),
                        pltpu.VMEM((MLA_HEADS, t, V_HEAD_DIM), F32)],
    )
    return pl.pallas_call(
        partial(_attn_kernel, t=t),
        grid_spec=grid_spec,
        out_shape=jax.ShapeDtypeStruct((b, s, MLA_HEADS * V_HEAD_DIM), BF16),
        compiler_params=_cparams(("parallel", "arbitrary")),
        name="attention",
    )(qi_tab, ki_tab, q, kn, kp, v)


def _latent_attn_kernel(q_ref, c_ref, kp_ref, wk_ref, wv_ref, o_ref, qs_ref, *, kv_len, heads_per_chunk):
    sq = q_ref.shape[2]
    sk = c_ref.shape[1]
    for h in range(MLA_HEADS):
        rows = slice(h * sq, (h + 1) * sq)
        qa = _dot_nt(q_ref[h, 0, :, :QK_NOPE_DIM], wk_ref[:, h * QK_NOPE_DIM:(h + 1) * QK_NOPE_DIM])
        qs_ref[rows, :KV_LORA] = qa.astype(BF16)
        qs_ref[rows, KV_LORA:] = q_ref[h, 0, :, QK_NOPE_DIM:]
    c = c_ref[0]
    keys = jnp.concatenate([c, kp_ref[0]], axis=1)
    rows_per_chunk = heads_per_chunk * sq
    keep = lax.broadcasted_iota(jnp.int32, (rows_per_chunk, sk), 1) < kv_len
    for j in range(MLA_HEADS // heads_per_chunk):
        s = _dot_nt(qs_ref[j * rows_per_chunk:(j + 1) * rows_per_chunk, :], keys)
        s = jnp.where(keep, s, NEG_BIG)
        p = jnp.exp2(s - jnp.max(s, axis=-1, keepdims=True))
        l = jnp.sum(p, axis=-1, keepdims=True)
        lat = (_dot(p.astype(BF16), c) / l).astype(BF16)
        for k in range(heads_per_chunk):
            h = j * heads_per_chunk + k
            o_ref[0, :, h * V_HEAD_DIM:(h + 1) * V_HEAD_DIM] = _dot(
                lat[k * sq:(k + 1) * sq], wv_ref[:, h * V_HEAD_DIM:(h + 1) * V_HEAD_DIM]).astype(o_ref.dtype)


def latent_attention(q, ckv, kp, w_uk, w_uv, *, kv_len):
    _, b, sq, _ = q.shape
    sk = ckv.shape[1]
    return pl.pallas_call(
        partial(_latent_attn_kernel, kv_len=kv_len, heads_per_chunk=4),
        grid=(b,),
        in_specs=[pl.BlockSpec((MLA_HEADS, 1, sq, Q_HEAD_W), lambda bi: (0, bi, 0, 0)),
                  pl.BlockSpec((1, sk, KV_LORA), lambda bi: (bi, 0, 0)),
                  pl.BlockSpec((1, sk, LANES), lambda bi: (bi, 0, 0)),
                  pl.BlockSpec(w_uk.shape, lambda bi: (0, 0)),
                  pl.BlockSpec(w_uv.shape, lambda bi: (0, 0))],
        out_specs=pl.BlockSpec((1, sq, MLA_HEADS * V_HEAD_DIM), lambda bi: (bi, 0, 0)),
        out_shape=jax.ShapeDtypeStruct((b, sq, MLA_HEADS * V_HEAD_DIM), BF16),
        scratch_shapes=[pltpu.VMEM((MLA_HEADS * sq, KV_LORA + LANES), BF16)],
        compiler_params=_cparams(("parallel",)),
        name="latent_attention",
    )(q, ckv, kp, w_uk, w_uv)


def _proj_res_kernel(x_ref, a_ref, w_ref, o_ref):
    for c, w in _col_chunks(w_ref.shape[1], COL_CHUNK):
        o_ref[:, c:c + w] = x_ref[:, c:c + w] + _dot(a_ref[...], w_ref[:, c:c + w])


def proj_res(x, a, w, *, tm):
    n, d = x.shape
    k = a.shape[1]
    return pl.pallas_call(
        _proj_res_kernel,
        grid=(n // tm,),
        in_specs=[pl.BlockSpec((tm, d), lambda i: (i, 0)),
                  pl.BlockSpec((tm, k), lambda i: (i, 0)),
                  pl.BlockSpec((k, d), lambda i: (0, 0))],
        out_specs=pl.BlockSpec((tm, d), lambda i: (i, 0)),
        out_shape=jax.ShapeDtypeStruct((n, d), F32),
        compiler_params=_cparams(("parallel",)),
        name="proj_res",
    )(x, a, w)


def _row_copy(src_ref, src_row, dst_ref, dst_row, sem):
    return pltpu.make_async_copy(src_ref.at[pl.ds(src_row, 1)], dst_ref.at[pl.ds(dst_row, 1)], sem)


def _dispatch_kernel(dest_ref, pad_ref, *rest, tm, tiles):
    x_refs = rest[:len(tiles)]
    g_ref, xs_ref, h_ref, z_ref, sem, zsem = rest[len(tiles):]
    i = pl.program_id(0)
    base = i * tm

    @pl.when(i == 0)
    def _():
        z_ref[...] = jnp.zeros_like(z_ref)
        for r in range(N_EXPERTS + 1):
            lo, hi = pad_ref[2 * r], pad_ref[2 * r + 1]

            def zissue(row, c):
                _row_copy(z_ref, 0, xs_ref, row, zsem).start()
                return c

            def zdrain(row, c):
                _row_copy(z_ref, 0, xs_ref, 0, zsem).wait()
                return c

            lax.fori_loop(lo, hi, zissue, 0)
            lax.fori_loop(lo, hi, zdrain, 0)

    buf = i % 2
    first = 0
    for x_ref, n_tiles in zip(x_refs, tiles):
        @pl.when(jnp.logical_and(i >= first, i < first + n_tiles))
        def _(x_ref=x_ref):
            h_ref[buf] = _rms(x_ref[...], g_ref[...])
        first += n_tiles

    def issue(t, c):
        for k in range(TOP_K):
            _row_copy(h_ref.at[buf], t, xs_ref, dest_ref[TOP_K * (base + t) + k], sem.at[buf]).start()
        return c

    lax.fori_loop(0, tm, issue, 0, unroll=8)

    def drain(b):
        def body(t, c):
            for k in range(TOP_K):
                _row_copy(h_ref.at[b], 0, xs_ref, 0, sem.at[b]).wait()
            return c

        lax.fori_loop(0, tm, body, 0, unroll=8)

    pl.when(i > 0)(lambda: drain(1 - buf))
    pl.when(i == pl.num_programs(0) - 1)(lambda: drain(buf))


def dispatch(dest, pad_rows, streams, g, n_slots, *, tm):
    d = streams[0].shape[1]
    tiles = [x.shape[0] // tm for x in streams]
    in_specs, first = [], 0
    for n_tiles in tiles:
        in_specs.append(pl.BlockSpec(
            (tm, d), lambda i, ds, pr, first=first, n_tiles=n_tiles: (jnp.clip(i - first, 0, n_tiles - 1), 0)))
        first += n_tiles
    in_specs.append(pl.BlockSpec((1, d), lambda i, ds, pr: (0, 0)))
    grid_spec = pltpu.PrefetchScalarGridSpec(
        num_scalar_prefetch=2,
        grid=(sum(tiles),),
        in_specs=in_specs,
        out_specs=pl.BlockSpec(memory_space=pl.ANY),
        scratch_shapes=[pltpu.VMEM((2, tm, d), F32), pltpu.VMEM((8, d), F32),
                        pltpu.SemaphoreType.DMA((2,)), pltpu.SemaphoreType.DMA],
    )
    return pl.pallas_call(
        partial(_dispatch_kernel, tm=tm, tiles=tuple(tiles)),
        grid_spec=grid_spec,
        out_shape=jax.ShapeDtypeStruct((n_slots, d), F32),
        compiler_params=_cparams(("arbitrary",)),
        name="moe_dispatch",
    )(dest, pad_rows, *streams, g.reshape(1, d))


def _combine_kernel(dest_ref, x_ref, gate_ref, gf_ref, yb_ref, o_ref, y_ref, sem, *, tm, final):
    i = pl.program_id(0)
    buf = i % 2

    def gather(tile, b):
        def issue(t, c):
            for k in range(TOP_K):
                _row_copy(yb_ref, dest_ref[TOP_K * (tile * tm + t) + k], y_ref.at[b, k], t, sem.at[b]).start()
            return c

        lax.fori_loop(0, tm, issue, 0, unroll=8)

    @pl.when(i == 0)
    def _():
        gather(0, 0)

    @pl.when(i + 1 < pl.num_programs(0))
    def _():
        gather(i + 1, 1 - buf)

    def drain(t, c):
        for k in range(TOP_K):
            _row_copy(yb_ref, 0, y_ref.at[buf, k], 0, sem.at[buf]).wait()
        return c

    lax.fori_loop(0, tm, drain, 0, unroll=8)
    g0 = gate_ref[:, 0:1]
    g1 = gate_ref[:, 1:2]
    y = x_ref[...] + (g0 * y_ref[buf, 0] + g1 * y_ref[buf, 1])
    o_ref[...] = _rms(y, gf_ref[...]) if final else y


def combine(dest, x, gates, yb, g_final, *, tm, final):
    n, d = x.shape
    grid_spec = pltpu.PrefetchScalarGridSpec(
        num_scalar_prefetch=1,
        grid=(n // tm,),
        in_specs=[pl.BlockSpec((tm, d), lambda i, ds: (i, 0)),
                  pl.BlockSpec((tm, LANES), lambda i, ds: (i, 0)),
                  pl.BlockSpec((1, d), lambda i, ds: (0, 0)),
                  pl.BlockSpec(memory_space=pl.ANY)],
        out_specs=pl.BlockSpec((tm, d), lambda i, ds: (i, 0)),
        scratch_shapes=[pltpu.VMEM((2, TOP_K, tm, d), F32), pltpu.SemaphoreType.DMA((2,))],
    )
    return pl.pallas_call(
        partial(_combine_kernel, tm=tm, final=final),
        grid_spec=grid_spec,
        out_shape=jax.ShapeDtypeStruct((n, d), F32),
        compiler_params=_cparams(("arbitrary",)),
        name="moe_combine",
    )(dest, x, gates, g_final.reshape(1, d), yb)


def moe(streams, routed, g, w_gate, w_up, w_down, g_final, *, tm_tok, tm_blk, tf, final):
    flat_e = jnp.concatenate([idx[:, :TOP_K].reshape(-1) for idx, _ in routed])
    n = flat_e.shape[0] // TOP_K
    d = streams[0].shape[1]
    onehot = (flat_e[:, None] == jnp.arange(N_EXPERTS, dtype=jnp.int32)[None, :]).astype(jnp.int32)
    csum = jnp.cumsum(onehot, axis=0)
    counts = csum[-1]
    padded = (counts + tm_blk - 1) // tm_blk * tm_blk
    pad_ends = jnp.cumsum(padded)
    pad_starts = pad_ends - padded
    dest = jnp.sum(onehot * (pad_starts[None, :] + csum - 1), axis=1).astype(jnp.int32)
    n_blocks = -(-(n * TOP_K + N_EXPERTS * (tm_blk - 1)) // tm_blk)
    block_start = jnp.arange(n_blocks, dtype=jnp.int32) * tm_blk
    block_expert = jnp.minimum(jnp.sum(pad_ends[None, :] <= block_start[:, None], axis=1),
                               N_EXPERTS - 1).astype(jnp.int32)
    n_used = (pad_ends[-1] // tm_blk).astype(jnp.int32).reshape(1)
    n_slots = n_blocks * tm_blk
    zero_lo = jnp.concatenate([pad_starts + counts, pad_ends[-1:]])
    zero_hi = jnp.concatenate([pad_ends, jnp.full((1,), n_slots, pad_ends.dtype)])
    pad_rows = jnp.stack([zero_lo, zero_hi], axis=1).reshape(-1).astype(jnp.int32)
    tm_tok = min([tm_tok] + [x.shape[0] for x in streams])
    slots = dispatch(dest, pad_rows, streams, g, n_slots, tm=tm_tok)
    dests, start = [], 0
    for x in streams:
        dests.append(dest[start:start + TOP_K * x.shape[0]])
        start += TOP_K * x.shape[0]
    yb = glu(slots, g, w_gate, w_up, w_down, block_expert, n_used, tm=tm_blk, tf=tf, dense=False)
    return [combine(dst, x, gates, yb, g_final, tm=min(tm_tok, x.shape[0]), final=final)
            for dst, x, (_, gates) in zip(dests, streams, routed)]


def _cast_kernel(w_ref, o_ref):
    o_ref[...] = w_ref[...].astype(o_ref.dtype)


def cast_group(w, first, count):
    _, r, c = w.shape
    tr = r
    while tr * c > CAST_BLOCK_ELEMS and tr % 32 == 0:
        tr //= 2
    return pl.pallas_call(
        _cast_kernel,
        grid=(count, r // tr),
        in_specs=[pl.BlockSpec((1, tr, c), lambda e, i: (first + e, i, 0))],
        out_specs=pl.BlockSpec((1, tr, c), lambda e, i: (e, i, 0)),
        out_shape=jax.ShapeDtypeStruct((count, r, c), BF16),
        compiler_params=_cparams(("parallel", "parallel")),
        name="cast_bf16",
    )(w)


def _final_norm_kernel(x_ref, g_ref, o_ref):
    o_ref[...] = _rms(x_ref[...], g_ref[...])


def final_norm(x, g, *, tm):
    n, d = x.shape
    return pl.pallas_call(
        _final_norm_kernel,
        grid=(n // tm,),
        in_specs=[pl.BlockSpec((tm, d), lambda i: (i, 0)), pl.BlockSpec((1, d), lambda i: (0, 0))],
        out_specs=pl.BlockSpec((tm, d), lambda i: (i, 0)),
        out_shape=jax.ShapeDtypeStruct((n, d), F32),
        compiler_params=_cparams(("parallel",)),
        name="final_norm",
    )(x, g.reshape(1, d))


def _rope_tables(pos):
    half = QK_ROPE_DIM // 2
    inv_freq = ROPE_THETA ** (-jnp.arange(half, dtype=F32) / half)
    ang = pos.astype(F32)[:, None] * inv_freq[None, :]
    cos = jnp.tile(jnp.cos(ang), (1, 2))
    sin = jnp.tile(jnp.sin(ang), (1, 2))
    z = jnp.zeros_like(cos)
    cat = lambda a, b: jnp.concatenate([a, b], axis=1)
    return (cat(cos, z), cat(z, cos), cat(sin, z), cat(z, sin), cat(cos, cos), cat(sin, sin))


def _rot_cols(w):
    half = QK_ROPE_DIM // 2
    return jnp.concatenate([-w[..., half:], w[..., :half]], axis=-1)


def _router_weights(w_router):
    w = jnp.pad(w_router, ((0, 0), (0, LANES - N_EXPERTS)))
    hi = w.astype(BF16)
    lo = (w - hi.astype(F32)).astype(BF16)
    return jnp.concatenate([hi, lo], axis=1)


def _mla_weights(w_uq, w_dkv):
    ql = w_uq.shape[0]
    wq = w_uq.reshape(ql, MLA_HEADS, QK_NOPE_DIM + QK_ROPE_DIM)
    nope, pe = wq[..., :QK_NOPE_DIM], wq[..., QK_NOPE_DIM:]
    z = jnp.zeros_like(pe)
    even = jnp.concatenate([nope, pe, z], axis=-1)
    odd = jnp.concatenate([nope, z, pe], axis=-1)
    is_even = (jnp.arange(MLA_HEADS) % 2 == 0)[None, :, None]
    w_uq_ext = jnp.where(is_even, even, odd).reshape(ql, MLA_HEADS * Q_HEAD_W).astype(BF16)
    w_uq_rot = _rot_cols(pe).reshape(ql, MLA_HEADS * QK_ROPE_DIM).astype(BF16)
    wc, wp = w_dkv[:, :KV_LORA], w_dkv[:, KV_LORA:]
    wr = _rot_cols(wp)
    w_dkv_ext = jnp.concatenate([wc, wp, wp, wr, wr], axis=1).astype(BF16)
    return w_uq_ext, w_uq_rot, w_dkv_ext


def kernel(x_prompt, x_sample, cache_l1_ckv, cache_l1_kpe, cache_l3_ckv, cache_l3_kpe, cache_mem_k, cache_mem_v, mem_prompt, g_mix, g_xattn, g_mem, g_ffn, g_final, sgu_w_in, sgu_b_in, sgu_ln_g, sgu_ln_b, sgu_w_s, sgu_b_s, sgu_w_out, mla_w_dq, mla_g_q, mla_w_uq, mla_w_dkv, mla_g_kv, mla_w_uk, mla_w_uv, mla_w_o, xa_w_q, xa_w_k, xa_w_v, xa_w_o, ffn_w_gate, ffn_w_up, ffn_w_down, moe_w_router, moe_w_gate, moe_w_up, moe_w_down):
    bp, sp, d = x_prompt.shape
    bs, ss, _ = x_sample.shape
    past = cache_l1_ckv.shape[1]
    depth = g_mix.shape[0]
    n_p, n_s = bp * sp, bs * ss
    mem_tokens = mem_prompt.shape[1]
    xa_w = XA_HEADS * XA_HEAD_DIM
    mla_cache = ((cache_l1_ckv, cache_l1_kpe), (cache_l3_ckv, cache_l3_kpe))

    bf = lambda a: a.astype(BF16)
    xp = x_prompt.reshape(n_p, d)
    xs = x_sample.reshape(n_s, d)
    tm_p, tm_s = min(ROW_TILE, n_p), min(ROW_TILE, n_s)
    tq_p = min(ROW_TILE, sp)
    tf = min(FF_TILE, ffn_w_gate.shape[2])

    mk_all, mv_all = mem_kv(mem_prompt.reshape(bp * mem_tokens, d), g_mem, bf(xa_w_k), bf(xa_w_v),
                            tm=min(ROW_TILE, bp * mem_tokens))
    new_mem_k = mk_all.reshape(depth, bp, mem_tokens, XA_HEADS, XA_HEAD_DIM)
    new_mem_v = mv_all.reshape(depth, bp, mem_tokens, XA_HEADS, XA_HEAD_DIM)

    tabs_p = _rope_tables(jnp.arange(sp, dtype=jnp.int32))
    tabs_s = _rope_tables(past + jnp.arange(ss, dtype=jnp.int32))

    ckv_p, kpe_p, ckv_s, kpe_s, sgu_v_s = [], [], [], [], []
    for layer in range(depth):
        i = layer // 2
        if layer % 2 == 0:
            w_in, w_out = bf(sgu_w_in[i]), bf(sgu_w_out[i])
            sgu_args = (w_in, sgu_b_in[i], sgu_ln_g[i], sgu_ln_b[i])
            zp, _ = sgu_in(xp, g_mix[layer], *sgu_args, tm=tm_p, emit_v=False)
            zs, v_new = sgu_in(xs, g_mix[layer], *sgu_args, tm=tm_s, emit_v=True)
            sgu_v_s.append(v_new.reshape(bs, ss, -1))
            b_col = sgu_b_s[i][:, :, None]
            xp = sgu_mix(xp, zp, sgu_w_s[i], b_col, w_out, tm=MIX_TILE, seg=SGU_CHUNK)
            rep = SGU_CHUNK // ss
            w_s_s = jnp.tile(sgu_w_s[i][:, :ss, :ss], (1, rep, rep))
            b_s_s = jnp.tile(sgu_b_s[i][:, :ss], (1, rep))[:, :, None]
            xs = sgu_mix(xs, zs, w_s_s, b_s_s, w_out, tm=MIX_TILE, seg=ss)
        else:
            w_uq_ext, w_uq_rot, w_dkv_ext = _mla_weights(mla_w_uq[i], mla_w_dkv[i])
            proj_w = (bf(mla_w_dq[i]), mla_g_q[i], w_uq_ext, w_uq_rot, w_dkv_ext, mla_g_kv[i])
            w_uk, w_uv, w_o = bf(mla_w_uk[i]), bf(mla_w_uv[i]), bf(mla_w_o[i])
            q, ckv, ckv_b, kpe = mla_proj(xp, g_mix[layer], *proj_w, tabs_p, tm=tq_p, tab_blocks=sp // tq_p)
            ckv_p.append(ckv.reshape(bp, sp, KV_LORA))
            kpe_p.append(kpe[:, :QK_ROPE_DIM].reshape(bp, sp, QK_ROPE_DIM))
            kn, vv = kv_up(ckv_b, w_uk, w_uv, tm=tm_p)
            hm = lambda a, b, s: a.reshape(MLA_HEADS, b, s, a.shape[-1])
            o = attention(hm(q, bp, sp), hm(kn, bp, sp), bf(kpe).reshape(bp, sp, LANES), hm(vv, bp, sp),
                          t=tq_p)
            xp = proj_res(xp, o.reshape(n_p, -1), w_o, tm=tm_p)
            q, ckv, ckv_b, kpe = mla_proj(xs, g_mix[layer], *proj_w, tabs_s, tm=ss, tab_blocks=1)
            ckv_s.append(ckv.reshape(bs, ss, KV_LORA))
            kpe_s.append(kpe[:, :QK_ROPE_DIM].reshape(bs, ss, QK_ROPE_DIM))
            c_ckv, c_kpe = mla_cache[i]
            kv_len = past + ss
            sk = -(-kv_len // LANES) * LANES
            pad = sk - kv_len
            ckv_all = jnp.concatenate([bf(c_ckv), ckv_b.reshape(bs, ss, KV_LORA),
                                       jnp.zeros((bs, pad, KV_LORA), BF16)], axis=1)
            kpe_all = jnp.concatenate([bf(jnp.tile(c_kpe, (1, 1, 2))), bf(kpe).reshape(bs, ss, LANES),
                                       jnp.zeros((bs, pad, LANES), BF16)], axis=1)
            o = latent_attention(hm(q, bs, ss), ckv_all, kpe_all, w_uk, w_uv, kv_len=kv_len)
            xs = proj_res(xs, o.reshape(n_s, -1), w_o, tm=tm_s)

        w_q, w_o = bf(xa_w_q[layer]), bf(xa_w_o[layer])
        is_moe = layer % 2 == 1
        router = (g_ffn[layer], _router_weights(moe_w_router[i])) if is_moe else None
        rp = xattn(xp.reshape(bp, sp, d), g_xattn[layer], w_q, bf(mk_all[layer]).reshape(bp, mem_tokens, xa_w),
                   bf(mv_all[layer]).reshape(bp, mem_tokens, xa_w), w_o, router, tm=tq_p)
        rs = xattn(xs.reshape(bs, ss, d), g_xattn[layer], w_q,
                   bf(cache_mem_k[layer]).reshape(bs, mem_tokens, xa_w),
                   bf(cache_mem_v[layer]).reshape(bs, mem_tokens, xa_w), w_o, router, tm=ss)
        xp, xs = rp[0].reshape(n_p, d), rs[0].reshape(n_s, d)

        if not is_moe:
            wg, wu, wd = (cast_group(w, i, 1) for w in (ffn_w_gate, ffn_w_up, ffn_w_down))
            ffn = lambda x, tm: glu(x, g_ffn[layer], wg, wu, wd, jnp.zeros((x.shape[0] // tm,), jnp.int32),
                                    jnp.full((1,), x.shape[0] // tm, jnp.int32), tm=tm, tf=tf, dense=True)
            xp, xs = ffn(xp, tm_p), ffn(xs, tm_s)
        else:
            wg, wu, wd = (cast_group(w.reshape((-1,) + w.shape[2:]), i * N_EXPERTS, N_EXPERTS)
                          for w in (moe_w_gate, moe_w_up, moe_w_down))
            routed = [(r[1].reshape(-1, LANES), r[2].reshape(-1, LANES)) for r in (rp, rs)]
            xp, xs = moe([xp, xs], routed, g_ffn[layer], wg, wu, wd, g_final, tm_tok=MOE_ROW_TILE, tm_blk=tm_p, tf=tf,
                         final=layer == depth - 1)

    if depth % 2 == 1:
        xp, xs = final_norm(xp, g_final, tm=tm_p), final_norm(xs, g_final, tm=tm_s)
    y_prompt = xp.reshape(bp, sp, d)
    y_sample = xs.reshape(bs, ss, d)
    return (y_prompt, y_sample,
            ckv_p[0], kpe_p[0], ckv_p[1], kpe_p[1],
            new_mem_k, new_mem_v,
            ckv_s[0], kpe_s[0], ckv_s[1], kpe_s[1],
            sgu_v_s[0], sgu_v_s[1])
```

```python
from functools import partial

import numpy as np
import jax
import jax.numpy as jnp
from jax import lax
from jax.experimental import pallas as pl
from jax.experimental.pallas import tpu as pltpu

F32 = jnp.float32
BF16 = jnp.bfloat16

CHUNK = 64
SGU_CHUNK = 128
SGU_GROUPS = 8
MLA_HEADS = 16
QK_NOPE_DIM = 128
QK_ROPE_DIM = 64
V_HEAD_DIM = 128
KV_LORA = 512
ROPE_THETA = 10000.0
XA_HEADS = 4
XA_HEAD_DIM = 128
N_EXPERTS = 8
TOP_K = 2
NORM_EPS = 1e-6

LANES = 128
BF16_SUBLANES = 16
Q_HEAD_W = 2 * LANES
VMEM_LIMIT = 56 * 1024 * 1024
NEG_BIG = -1e30
ROW_TILE = 512
MIX_TILE = 256
MOE_ROW_TILE = 512
FF_TILE = 1024
COL_CHUNK = 512
CAST_BLOCK_ELEMS = 1024 * 1024
HEADS_PER_ITER = 16


def _cparams(sem):
    return pltpu.CompilerParams(dimension_semantics=sem, vmem_limit_bytes=VMEM_LIMIT)


def _rms(x, g):
    return x * lax.rsqrt(jnp.mean(x * x, axis=-1, keepdims=True) + NORM_EPS) * g


def _dot(a, b):
    return jnp.dot(a, b, preferred_element_type=F32)


def _dot_nt(a, b):
    return lax.dot_general(a, b, (((1,), (1,)), ((), ())), preferred_element_type=F32)


def _col_chunks(n, width):
    return [(c, min(width, n - c)) for c in range(0, n, width)]


def _sgu_in_kernel(x_ref, g_ref, w_ref, b_ref, lng_ref, lnb_ref, z_ref, *rest, emit_v):
    if emit_v:
        v_ref, zs_ref = rest
    else:
        (zs_ref,) = rest
    j = pl.program_id(0)
    h = _rms(x_ref[...], g_ref[...]).astype(BF16)
    for c, w in _col_chunks(w_ref.shape[1], COL_CHUNK):
        a = _dot(h, w_ref[:, c:c + w]) + b_ref[:, c:c + w]
        zs_ref[:, c:c + w] = 0.5 * a * (1.0 + lax.erf(a * np.float32(np.sqrt(0.5))))

    @pl.when(j == 0)
    def _():
        z_ref[...] = zs_ref[...].astype(z_ref.dtype)

    @pl.when(j == 1)
    def _():
        z = zs_ref[...]
        mu = jnp.mean(z, axis=-1, keepdims=True)
        zc = z - mu
        var = jnp.mean(zc * zc, axis=-1, keepdims=True)
        v = zc * lax.rsqrt(var + NORM_EPS) * lng_ref[...] + lnb_ref[...]
        z_ref[...] = v.astype(z_ref.dtype)
        if emit_v:
            v_ref[...] = v


def sgu_in(x, g, w_in, b_in, ln_g, ln_b, *, tm, emit_v):
    n, d = x.shape
    ds = w_in.shape[1] // 2
    nt = n // tm
    out_shape = [jax.ShapeDtypeStruct((n, 2 * ds), BF16)]
    out_specs = [pl.BlockSpec((tm, ds), lambda j, i: (i, j))]
    if emit_v:
        out_shape.append(jax.ShapeDtypeStruct((n, ds), F32))
        out_specs.append(pl.BlockSpec((tm, ds), lambda j, i: (i * j, 0)))
    res = pl.pallas_call(
        partial(_sgu_in_kernel, emit_v=emit_v),
        grid=(2, nt),
        in_specs=[
            pl.BlockSpec((tm, d), lambda j, i: (i, 0)),
            pl.BlockSpec((1, d), lambda j, i: (0, 0)),
            pl.BlockSpec((d, ds), lambda j, i: (0, j)),
            pl.BlockSpec((1, ds), lambda j, i: (0, j)),
            pl.BlockSpec((1, ds), lambda j, i: (0, 0)),
            pl.BlockSpec((1, ds), lambda j, i: (0, 0)),
        ],
        out_specs=out_specs,
        out_shape=out_shape,
        scratch_shapes=[pltpu.VMEM((tm, ds), F32)],
        compiler_params=_cparams(("arbitrary", "arbitrary")),
        name="sgu_in",
    )(x, g.reshape(1, d), w_in, b_in.reshape(1, -1), ln_g.reshape(1, ds), ln_b.reshape(1, ds))
    return res if emit_v else (res[0], None)


def _sgu_mix_kernel(x_ref, u_ref, v_ref, ws_ref, bs_ref, wo_ref, o_ref, gs_ref, *, seg):
    tm = x_ref.shape[0]
    gd = v_ref.shape[1] // SGU_GROUPS
    row = lax.broadcasted_iota(jnp.int32, (SGU_CHUNK, SGU_CHUNK), 0)
    col = lax.broadcasted_iota(jnp.int32, (SGU_CHUNK, SGU_CHUNK), 1)
    keep = (col <= row) & ((col // seg) == (row // seg))
    for g in range(SGU_GROUPS):
        wg = jnp.where(keep, ws_ref[g], 0.0).astype(BF16)
        bg = bs_ref[g]
        for c in range(tm // SGU_CHUNK):
            rs = slice(c * SGU_CHUNK, (c + 1) * SGU_CHUNK)
            cs = slice(g * gd, (g + 1) * gd)
            mixed = _dot(wg, v_ref[rs, cs]) + bg
            gs_ref[rs, cs] = (u_ref[rs, cs].astype(F32) * mixed).astype(BF16)
    for c, w in _col_chunks(wo_ref.shape[1], COL_CHUNK):
        o_ref[:, c:c + w] = x_ref[:, c:c + w] + _dot(gs_ref[...], wo_ref[:, c:c + w])


def sgu_mix(x, z, w_s, b_s, w_out, *, tm, seg):
    n, d = x.shape
    ds = z.shape[1] // 2
    return pl.pallas_call(
        partial(_sgu_mix_kernel, seg=seg),
        grid=(n // tm,),
        in_specs=[
            pl.BlockSpec((tm, d), lambda i: (i, 0)),
            pl.BlockSpec((tm, ds), lambda i: (i, 0)),
            pl.BlockSpec((tm, ds), lambda i: (i, 1)),
            pl.BlockSpec(w_s.shape, lambda i: (0, 0, 0)),
            pl.BlockSpec(b_s.shape, lambda i: (0, 0, 0)),
            pl.BlockSpec(w_out.shape, lambda i: (0, 0)),
        ],
        out_specs=pl.BlockSpec((tm, d), lambda i: (i, 0)),
        out_shape=jax.ShapeDtypeStruct((n, d), F32),
        scratch_shapes=[pltpu.VMEM((tm, ds), BF16)],
        compiler_params=_cparams(("parallel",)),
        name="sgu_mix",
    )(x, z, z, w_s, b_s, w_out)


def _mem_kv_kernel(m_ref, g_ref, wk_ref, wv_ref, k_ref, v_ref):
    h = _rms(m_ref[...], g_ref[0]).astype(BF16)
    k_ref[0] = _dot(h, wk_ref[0])
    v_ref[0] = _dot(h, wv_ref[0])


def mem_kv(mem, g_mem, w_k, w_v, *, tm):
    r, d = mem.shape
    nl, _, w = w_k.shape
    return pl.pallas_call(
        _mem_kv_kernel,
        grid=(nl, r // tm),
        in_specs=[
            pl.BlockSpec((tm, d), lambda l, i: (i, 0)),
            pl.BlockSpec((1, 1, d), lambda l, i: (l, 0, 0)),
            pl.BlockSpec((1, d, w), lambda l, i: (l, 0, 0)),
            pl.BlockSpec((1, d, w), lambda l, i: (l, 0, 0)),
        ],
        out_specs=[pl.BlockSpec((1, tm, w), lambda l, i: (l, i, 0))] * 2,
        out_shape=[jax.ShapeDtypeStruct((nl, r, w), F32)] * 2,
        compiler_params=_cparams(("parallel", "parallel")),
        name="mem_kv",
    )(mem, g_mem.reshape(nl, 1, d), w_k, w_v)


def _top2_route(h, w_hl):
    h_hi = h.astype(BF16)
    h_lo = (h - h_hi.astype(F32)).astype(BF16)
    a = _dot(h_hi, w_hl[...])
    logits = a[:, :LANES] + a[:, LANES:] + _dot(h_lo, w_hl[:, :LANES])
    lane_i = lax.broadcasted_iota(jnp.int32, logits.shape, 1)
    lane = lane_i.astype(F32)
    logits = jnp.where(lane_i < N_EXPERTS, logits, -jnp.inf)
    m1 = jnp.max(logits, axis=-1, keepdims=True)
    i1 = jnp.min(jnp.where(logits == m1, lane, float(LANES)), axis=-1, keepdims=True)
    rest = jnp.where(lane == i1, -jnp.inf, logits)
    m2 = jnp.max(rest, axis=-1, keepdims=True)
    i2 = jnp.min(jnp.where(rest == m2, lane, float(LANES)), axis=-1, keepdims=True)
    e2 = jnp.exp(m2 - m1)
    g1 = 1.0 / (1.0 + e2)
    g2 = e2 / (1.0 + e2)
    idx = jnp.where(lane_i == 0, i1, jnp.where(lane_i == 1, i2, 0.0)).astype(jnp.int32)
    gates = jnp.where(lane_i == 0, g1, jnp.where(lane_i == 1, g2, 0.0))
    return idx, gates


def _xattn_kernel(x_ref, g_ref, wq_ref, k_ref, v_ref, wo_ref, *rest, route):
    if route:
        gf_ref, wr_ref, o_ref, idx_ref, gate_ref, os_ref = rest
    else:
        o_ref, os_ref = rest
    x = x_ref[0]
    h = _rms(x, g_ref[...]).astype(BF16)
    q = (_dot(h, wq_ref[...]) * np.float32(XA_HEAD_DIM ** -0.5)).astype(BF16)
    for hd in range(XA_HEADS):
        cs = slice(hd * XA_HEAD_DIM, (hd + 1) * XA_HEAD_DIM)
        s = _dot_nt(q[:, cs], k_ref[0, :, cs])
        p = jnp.exp(s - jnp.max(s, axis=-1, keepdims=True))
        l = jnp.sum(p, axis=-1, keepdims=True)
        os_ref[:, cs] = (_dot(p.astype(BF16), v_ref[0, :, cs]) / l).astype(BF16)
    for c, w in _col_chunks(wo_ref.shape[1], COL_CHUNK):
        o_ref[0, :, c:c + w] = x[:, c:c + w] + _dot(os_ref[...], wo_ref[:, c:c + w])
    if route:
        idx_ref[0], gate_ref[0] = _top2_route(_rms(o_ref[0], gf_ref[...]), wr_ref)


def xattn(x, g, w_q, mem_k, mem_v, w_o, router=None, *, tm):
    b, s, d = x.shape
    m, xw = mem_k.shape[1:]
    row_spec = lambda w: pl.BlockSpec((1, tm, w), lambda bi, i: (bi, i, 0))
    in_specs = [
        row_spec(d),
        pl.BlockSpec((1, d), lambda bi, i: (0, 0)),
        pl.BlockSpec((d, xw), lambda bi, i: (0, 0)),
        pl.BlockSpec((1, m, xw), lambda bi, i: (bi, 0, 0)),
        pl.BlockSpec((1, m, xw), lambda bi, i: (bi, 0, 0)),
        pl.BlockSpec((xw, d), lambda bi, i: (0, 0)),
    ]
    args = [x, g.reshape(1, d), w_q, mem_k, mem_v, w_o]
    out_specs, out_shape = [row_spec(d)], [jax.ShapeDtypeStruct((b, s, d), F32)]
    if router is not None:
        in_specs += [pl.BlockSpec((1, d), lambda bi, i: (0, 0)), pl.BlockSpec((d, 2 * LANES), lambda bi, i: (0, 0))]
        args += [router[0].reshape(1, d), router[1]]
        out_specs += [row_spec(LANES), row_spec(LANES)]
        out_shape += [jax.ShapeDtypeStruct((b, s, LANES), jnp.int32), jax.ShapeDtypeStruct((b, s, LANES), F32)]
    return pl.pallas_call(
        partial(_xattn_kernel, route=router is not None),
        grid=(b, s // tm),
        in_specs=in_specs,
        out_specs=out_specs,
        out_shape=out_shape,
        scratch_shapes=[pltpu.VMEM((tm, xw), BF16)],
        compiler_params=_cparams(("parallel", "parallel")),
        name="xattn",
    )(*args)


def _glu_kernel(be_ref, nb_ref, x_ref, g_ref, wg_ref, wu_ref, wd_ref, o_ref, h_ref, *, dense):
    i = pl.program_id(0)
    f = pl.program_id(1)

    @pl.when(i < nb_ref[0])
    def _():
        @pl.when(f == 0)
        def _():
            if dense:
                x = x_ref[...]
                h_ref[...] = _rms(x, g_ref[...]).astype(BF16)
                o_ref[...] = x
            else:
                h_ref[...] = x_ref[...].astype(BF16)
                o_ref[...] = jnp.zeros_like(o_ref)

        h = h_ref[...]
        a = _dot(h, wg_ref[0])
        b = _dot(h, wu_ref[0])
        hid = (a * jax.nn.sigmoid(a) * b).astype(BF16)
        o_ref[...] += _dot(hid, wd_ref[0])

    @pl.when(jnp.logical_and(i >= nb_ref[0], f == 0))
    def _():
        o_ref[...] = jnp.zeros_like(o_ref)


def glu(x, g, w_gate, w_up, w_down, block_expert, n_blocks_used, *, tm, tf, dense):
    r, d = x.shape
    ff = w_gate.shape[2]
    nf = ff // tf

    def wsel(i, f, be, nb):
        live = i < nb[0]
        return be[i], jnp.where(live, f, nf - 1)

    grid_spec = pltpu.PrefetchScalarGridSpec(
        num_scalar_prefetch=2,
        grid=(r // tm, nf),
        in_specs=[
            pl.BlockSpec((tm, d), lambda i, f, be, nb: (jnp.minimum(i, nb[0] - 1), 0)),
            pl.BlockSpec((1, d), lambda i, f, be, nb: (0, 0)),
            pl.BlockSpec((1, d, tf), lambda i, f, be, nb: (wsel(i, f, be, nb)[0], 0, wsel(i, f, be, nb)[1])),
            pl.BlockSpec((1, d, tf), lambda i, f, be, nb: (wsel(i, f, be, nb)[0], 0, wsel(i, f, be, nb)[1])),
            pl.BlockSpec((1, tf, d), lambda i, f, be, nb: (wsel(i, f, be, nb)[0], wsel(i, f, be, nb)[1], 0)),
        ],
        out_specs=pl.BlockSpec((tm, d), lambda i, f, be, nb: (i, 0)),
        scratch_shapes=[pltpu.VMEM((tm, d), BF16)],
    )
    return pl.pallas_call(
        partial(_glu_kernel, dense=dense),
        grid_spec=grid_spec,
        out_shape=jax.ShapeDtypeStruct((r, d), F32),
        compiler_params=_cparams(("arbitrary", "arbitrary")),
        name="glu_dense" if dense else "glu_expert",
    )(block_expert, n_blocks_used, x, g.reshape(1, d), w_gate, w_up, w_down)


def _mla_proj_kernel(x_ref, g_ref, wdq_ref, gq_ref, wuq_ref, wrot_ref, wkv_ref, gkv_ref,
                     ce_ref, co_ref, se_ref, so_ref, ck_ref, sk_ref,
                     q_ref, ckv_ref, ckvb_ref, kpe_ref, cq_ref):
    h = _rms(x_ref[...], g_ref[...]).astype(BF16)
    cq_ref[...] = _rms(_dot(h, wdq_ref[...]), gq_ref[...]).astype(BF16)
    kv = _dot(h, wkv_ref[...])
    ckv = _rms(kv[:, :KV_LORA], gkv_ref[...])
    ckv_ref[...] = ckv
    ckvb_ref[...] = ckv.astype(BF16)
    kpe_ref[...] = (kv[:, KV_LORA:KV_LORA + LANES] * ck_ref[...]
                    + kv[:, KV_LORA + LANES:KV_LORA + 2 * LANES] * sk_ref[...])
    scale = np.float32((QK_NOPE_DIM + QK_ROPE_DIM) ** -0.5 * np.log2(np.e))
    cq = cq_ref[...]
    for p in range(MLA_HEADS // 2):
        a = _dot(cq, wuq_ref[:, 2 * p * Q_HEAD_W:(2 * p + 2) * Q_HEAD_W])
        r = _dot(cq, wrot_ref[:, p * LANES:(p + 1) * LANES])
        for k, (c_ref, s_ref) in enumerate(((ce_ref, se_ref), (co_ref, so_ref))):
            q_ref[2 * p + k, :, :LANES] = (a[:, k * Q_HEAD_W:k * Q_HEAD_W + LANES] * scale).astype(BF16)
            pe = a[:, k * Q_HEAD_W + LANES:(k + 1) * Q_HEAD_W] * c_ref[...] + r * s_ref[...]
            q_ref[2 * p + k, :, LANES:] = (pe * scale).astype(BF16)


def mla_proj(x, g, w_dq, g_q, w_uq_ext, w_uq_rot, w_dkv_ext, g_kv, tabs, *, tm, tab_blocks):
    n, d = x.shape
    ql = w_dq.shape[1]
    full = lambda a: pl.BlockSpec(a.shape, lambda i: (0,) * a.ndim)
    tab_spec = pl.BlockSpec((tm, LANES), lambda i: (i % tab_blocks, 0))
    return pl.pallas_call(
        _mla_proj_kernel,
        grid=(n // tm,),
        in_specs=[pl.BlockSpec((tm, d), lambda i: (i, 0)), pl.BlockSpec((1, d), lambda i: (0, 0)),
                  full(w_dq), pl.BlockSpec((1, ql), lambda i: (0, 0)), full(w_uq_ext), full(w_uq_rot),
                  full(w_dkv_ext), pl.BlockSpec((1, KV_LORA), lambda i: (0, 0))] + [tab_spec] * 6,
        out_specs=[pl.BlockSpec((MLA_HEADS, tm, Q_HEAD_W), lambda i: (0, i, 0)),
                   pl.BlockSpec((tm, KV_LORA), lambda i: (i, 0)),
                   pl.BlockSpec((tm, KV_LORA), lambda i: (i, 0)),
                   pl.BlockSpec((tm, LANES), lambda i: (i, 0))],
        out_shape=[jax.ShapeDtypeStruct((MLA_HEADS, n, Q_HEAD_W), BF16),
                   jax.ShapeDtypeStruct((n, KV_LORA), F32),
                   jax.ShapeDtypeStruct((n, KV_LORA), BF16),
                   jax.ShapeDtypeStruct((n, LANES), F32)],
        scratch_shapes=[pltpu.VMEM((tm, ql), BF16)],
        compiler_params=_cparams(("parallel",)),
        name="mla_proj",
    )(x, g.reshape(1, d), w_dq, g_q.reshape(1, ql), w_uq_ext, w_uq_rot, w_dkv_ext,
      g_kv.reshape(1, KV_LORA), *tabs)


def _kv_up_kernel(c_ref, wk_ref, wv_ref, k_ref, v_ref):
    c = c_ref[...]
    for w_ref, o_ref in ((wk_ref, k_ref), (wv_ref, v_ref)):
        for p in range(MLA_HEADS // 2):
            r = _dot(c, w_ref[:, 2 * p * LANES:(2 * p + 2) * LANES]).astype(BF16)
            o_ref[2 * p] = r[:, :LANES]
            o_ref[2 * p + 1] = r[:, LANES:]


def kv_up(ckv, w_uk, w_uv, *, tm):
    n, c = ckv.shape
    w = w_uk.shape[1]
    return pl.pallas_call(
        _kv_up_kernel,
        grid=(n // tm,),
        in_specs=[pl.BlockSpec((tm, c), lambda i: (i, 0)),
                  pl.BlockSpec((c, w), lambda i: (0, 0)),
                  pl.BlockSpec((c, w), lambda i: (0, 0))],
        out_specs=[pl.BlockSpec((MLA_HEADS, tm, LANES), lambda i: (0, i, 0))] * 2,
        out_shape=[jax.ShapeDtypeStruct((MLA_HEADS, n, LANES), BF16)] * 2,
        compiler_params=_cparams(("parallel",)),
        name="kv_up",
    )(ckv, w_uk, w_uv)


def _attn_kernel(qi_ref, ki_ref, q_ref, kn_ref, kp_ref, v_ref, *rest, t, n_cast):
    w_refs = rest[:n_cast]
    o_ref = rest[n_cast]
    wo_refs = rest[n_cast + 1:2 * n_cast + 1]
    m_ref, l_ref, acc_ref = rest[2 * n_cast + 1:]
    for w_ref, wo_ref in zip(w_refs, wo_refs):
        wo_ref[...] = w_ref[...].astype(wo_ref.dtype)
    tq = tk = t
    step_id = pl.program_id(1)
    qi = qi_ref[step_id]
    ki = ki_ref[step_id]

    @pl.when(ki == 0)
    def _():
        m_ref[...] = jnp.full_like(m_ref, NEG_BIG)
        l_ref[...] = jnp.zeros_like(l_ref)
        acc_ref[...] = jnp.zeros_like(acc_ref)

    def run(masked):
        kp = kp_ref[0]
        ones = jnp.ones((tk, LANES), BF16)
        keep = None
        if masked:
            qc = lax.broadcasted_iota(jnp.int32, (tq, tk), 0) // CHUNK
            kc = lax.broadcasted_iota(jnp.int32, (tq, tk), 1) // CHUNK
            keep = kc <= qc

        def head(h):
            k = jnp.concatenate([kn_ref[h, 0], kp], axis=1)
            s = _dot_nt(q_ref[h, 0], k)
            if keep is not None:
                s = jnp.where(keep, s, NEG_BIG)
            m_old = m_ref[h]
            m_new = jnp.maximum(m_old, jnp.max(s, axis=-1, keepdims=True))
            alpha = jnp.exp2(m_old - m_new)
            p = jnp.exp2(s - jnp.tile(m_new, (1, tk // LANES))).astype(BF16)
            pv = _dot(p, jnp.concatenate([v_ref[h, 0], ones], axis=1))
            l_ref[h] = alpha * l_ref[h] + pv[:, V_HEAD_DIM:]
            acc_ref[h] = alpha * acc_ref[h] + pv[:, :V_HEAD_DIM]
            m_ref[h] = m_new

        def head_group(j, c):
            for k in range(HEADS_PER_ITER):
                head(HEADS_PER_ITER * j + k)
            return c

        lax.fori_loop(0, MLA_HEADS // HEADS_PER_ITER, head_group, 0)

    pl.when(ki < qi)(lambda: run(False))
    pl.when(ki == qi)(lambda: run(True))

    @pl.when(ki == qi)
    def _():
        for h in range(MLA_HEADS):
            o_ref[0, :, h * V_HEAD_DIM:(h + 1) * V_HEAD_DIM] = (acc_ref[h] / l_ref[h]).astype(o_ref.dtype)


def _cast_row_block(r, count, n_steps):
    rb = BF16_SUBLANES
    while count * r // rb > n_steps:
        rb *= 2
    return rb if r % rb == 0 else None


def _attn_steps(b, s, t):
    return b * (s // t) * (s // t + 1) // 2


def attention(q, kn, kp, v, cast, *, t):
    _, b, s, _ = q.shape
    assert s % t == 0 and t % CHUNK == 0
    pairs = [(qi, ki) for qi in range(s // t) for ki in range(qi + 1)]
    qi_tab = jnp.asarray(np.array([p[0] for p in pairs], np.int32))
    ki_tab = jnp.asarray(np.array([p[1] for p in pairs], np.int32))
    n_steps = b * len(pairs)
    step = lambda bi, s: bi * len(pairs) + s
    w_args, w_in_specs, w_out_specs, w_out_shapes = [], [], [], []
    for w, first, count in cast:
        _, r, c = w.shape
        rows = count * r
        rb = _cast_row_block(r, count, n_steps)
        nblk = rows // rb
        w_args.append(w.reshape(-1, c))
        w_in_specs.append(pl.BlockSpec(
            (rb, c), lambda bi, s, qt, kt, o=first * r // rb, n=nblk: (o + jnp.minimum(step(bi, s), n - 1), 0)))
        w_out_specs.append(pl.BlockSpec((rb, c), lambda bi, s, qt, kt, n=nblk: (jnp.minimum(step(bi, s), n - 1), 0)))
        w_out_shapes.append(jax.ShapeDtypeStruct((rows, c), BF16))
    grid_spec = pltpu.PrefetchScalarGridSpec(
        num_scalar_prefetch=2,
        grid=(b, len(pairs)),
        in_specs=[
            pl.BlockSpec((MLA_HEADS, 1, t, Q_HEAD_W), lambda bi, s, qt, kt: (0, bi, qt[s], 0)),
            pl.BlockSpec((MLA_HEADS, 1, t, QK_NOPE_DIM), lambda bi, s, qt, kt: (0, bi, kt[s], 0)),
            pl.BlockSpec((1, t, LANES), lambda bi, s, qt, kt: (bi, kt[s], 0)),
            pl.BlockSpec((MLA_HEADS, 1, t, V_HEAD_DIM), lambda bi, s, qt, kt: (0, bi, kt[s], 0)),
        ] + w_in_specs,
        out_specs=[pl.BlockSpec((1, t, MLA_HEADS * V_HEAD_DIM), lambda bi, s, qt, kt: (bi, qt[s], 0))] + w_out_specs,
        scratch_shapes=[pltpu.VMEM((MLA_HEADS, t, LANES), F32), pltpu.VMEM((MLA_HEADS, t, LANES), F32),
                        pltpu.VMEM((MLA_HEADS, t, V_HEAD_DIM), F32)],
    )
    res = pl.pallas_call(
        partial(_attn_kernel, t=t, n_cast=len(cast)),
        grid_spec=grid_spec,
        out_shape=[jax.ShapeDtypeStruct((b, s, MLA_HEADS * V_HEAD_DIM), BF16)] + w_out_shapes,
        compiler_params=_cparams(("arbitrary", "arbitrary")),
        name="attention",
    )(qi_tab, ki_tab, q, kn, kp, v, *w_args)
    return res[0], [o.reshape(count, w.shape[1], w.shape[2]) for o, (w, _, count) in zip(res[1:], cast)]


def _latent_attn_kernel(q_ref, c_ref, kp_ref, wk_ref, wv_ref, o_ref, qs_ref, *, kv_len, heads_per_chunk):
    sq = q_ref.shape[2]
    sk = c_ref.shape[1]
    for h in range(MLA_HEADS):
        rows = slice(h * sq, (h + 1) * sq)
        qa = _dot_nt(q_ref[h, 0, :, :QK_NOPE_DIM], wk_ref[:, h * QK_NOPE_DIM:(h + 1) * QK_NOPE_DIM])
        qs_ref[rows, :KV_LORA] = qa.astype(BF16)
        qs_ref[rows, KV_LORA:] = q_ref[h, 0, :, QK_NOPE_DIM:]
    c = c_ref[0]
    keys = jnp.concatenate([c, kp_ref[0]], axis=1)
    rows_per_chunk = heads_per_chunk * sq
    keep = lax.broadcasted_iota(jnp.int32, (rows_per_chunk, sk), 1) < kv_len
    for j in range(MLA_HEADS // heads_per_chunk):
        s = _dot_nt(qs_ref[j * rows_per_chunk:(j + 1) * rows_per_chunk, :], keys)
        s = jnp.where(keep, s, NEG_BIG)
        p = jnp.exp2(s - jnp.max(s, axis=-1, keepdims=True))
        l = jnp.sum(p, axis=-1, keepdims=True)
        lat = (_dot(p.astype(BF16), c) / l).astype(BF16)
        for k in range(heads_per_chunk):
            h = j * heads_per_chunk + k
            o_ref[0, :, h * V_HEAD_DIM:(h + 1) * V_HEAD_DIM] = _dot(
                lat[k * sq:(k + 1) * sq], wv_ref[:, h * V_HEAD_DIM:(h + 1) * V_HEAD_DIM]).astype(o_ref.dtype)


def latent_attention(q, ckv, kp, w_uk, w_uv, *, kv_len):
    _, b, sq, _ = q.shape
    sk = ckv.shape[1]
    return pl.pallas_call(
        partial(_latent_attn_kernel, kv_len=kv_len, heads_per_chunk=4),
        grid=(b,),
        in_specs=[pl.BlockSpec((MLA_HEADS, 1, sq, Q_HEAD_W), lambda bi: (0, bi, 0, 0)),
                  pl.BlockSpec((1, sk, KV_LORA), lambda bi: (bi, 0, 0)),
                  pl.BlockSpec((1, sk, LANES), lambda bi: (bi, 0, 0)),
                  pl.BlockSpec(w_uk.shape, lambda bi: (0, 0)),
                  pl.BlockSpec(w_uv.shape, lambda bi: (0, 0))],
        out_specs=pl.BlockSpec((1, sq, MLA_HEADS * V_HEAD_DIM), lambda bi: (bi, 0, 0)),
        out_shape=jax.ShapeDtypeStruct((b, sq, MLA_HEADS * V_HEAD_DIM), BF16),
        scratch_shapes=[pltpu.VMEM((MLA_HEADS * sq, KV_LORA + LANES), BF16)],
        compiler_params=_cparams(("parallel",)),
        name="latent_attention",
    )(q, ckv, kp, w_uk, w_uv)


def _proj_res_kernel(x_ref, a_ref, w_ref, o_ref):
    for c, w in _col_chunks(w_ref.shape[1], COL_CHUNK):
        o_ref[:, c:c + w] = x_ref[:, c:c + w] + _dot(a_ref[...], w_ref[:, c:c + w])


def proj_res(x, a, w, *, tm):
    n, d = x.shape
    k = a.shape[1]
    return pl.pallas_call(
        _proj_res_kernel,
        grid=(n // tm,),
        in_specs=[pl.BlockSpec((tm, d), lambda i: (i, 0)),
                  pl.BlockSpec((tm, k), lambda i: (i, 0)),
                  pl.BlockSpec((k, d), lambda i: (0, 0))],
        out_specs=pl.BlockSpec((tm, d), lambda i: (i, 0)),
        out_shape=jax.ShapeDtypeStruct((n, d), F32),
        compiler_params=_cparams(("parallel",)),
        name="proj_res",
    )(x, a, w)


def _row_copy(src_ref, src_row, dst_ref, dst_row, sem):
    return pltpu.make_async_copy(src_ref.at[pl.ds(src_row, 1)], dst_ref.at[pl.ds(dst_row, 1)], sem)


def _dispatch_kernel(dest_ref, pad_ref, *rest, tm, tiles):
    x_refs = rest[:len(tiles)]
    g_ref, xs_ref, h_ref, z_ref, sem, zsem = rest[len(tiles):]
    i = pl.program_id(0)
    base = i * tm

    @pl.when(i == 0)
    def _():
        z_ref[...] = jnp.zeros_like(z_ref)
        for r in range(N_EXPERTS + 1):
            lo, hi = pad_ref[2 * r], pad_ref[2 * r + 1]

            def zissue(row, c):
                _row_copy(z_ref, 0, xs_ref, row, zsem).start()
                return c

            def zdrain(row, c):
                _row_copy(z_ref, 0, xs_ref, 0, zsem).wait()
                return c

            lax.fori_loop(lo, hi, zissue, 0)
            lax.fori_loop(lo, hi, zdrain, 0)

    buf = i % 2
    first = 0
    for x_ref, n_tiles in zip(x_refs, tiles):
        @pl.when(jnp.logical_and(i >= first, i < first + n_tiles))
        def _(x_ref=x_ref):
            h_ref[buf] = _rms(x_ref[...], g_ref[...])
        first += n_tiles

    def issue(t, c):
        for k in range(TOP_K):
            _row_copy(h_ref.at[buf], t, xs_ref, dest_ref[TOP_K * (base + t) + k], sem.at[buf]).start()
        return c

    lax.fori_loop(0, tm, issue, 0, unroll=8)

    def drain(b):
        def body(t, c):
            for k in range(TOP_K):
                _row_copy(h_ref.at[b], 0, xs_ref, 0, sem.at[b]).wait()
            return c

        lax.fori_loop(0, tm, body, 0, unroll=8)

    pl.when(i > 0)(lambda: drain(1 - buf))
    pl.when(i == pl.num_programs(0) - 1)(lambda: drain(buf))


def dispatch(dest, pad_rows, streams, g, n_slots, *, tm):
    d = streams[0].shape[1]
    tiles = [x.shape[0] // tm for x in streams]
    in_specs, first = [], 0
    for n_tiles in tiles:
        in_specs.append(pl.BlockSpec(
            (tm, d), lambda i, ds, pr, first=first, n_tiles=n_tiles: (jnp.clip(i - first, 0, n_tiles - 1), 0)))
        first += n_tiles
    in_specs.append(pl.BlockSpec((1, d), lambda i, ds, pr: (0, 0)))
    grid_spec = pltpu.PrefetchScalarGridSpec(
        num_scalar_prefetch=2,
        grid=(sum(tiles),),
        in_specs=in_specs,
        out_specs=pl.BlockSpec(memory_space=pl.ANY),
        scratch_shapes=[pltpu.VMEM((2, tm, d), F32), pltpu.VMEM((8, d), F32),
                        pltpu.SemaphoreType.DMA((2,)), pltpu.SemaphoreType.DMA],
    )
    return pl.pallas_call(
        partial(_dispatch_kernel, tm=tm, tiles=tuple(tiles)),
        grid_spec=grid_spec,
        out_shape=jax.ShapeDtypeStruct((n_slots, d), F32),
        compiler_params=_cparams(("arbitrary",)),
        name="moe_dispatch",
    )(dest, pad_rows, *streams, g.reshape(1, d))


def _combine_kernel(dest_ref, x_ref, gate_ref, gf_ref, yb_ref, o_ref, y_ref, sem, *, tm, final):
    i = pl.program_id(0)
    buf = i % 2

    def gather(tile, b):
        def issue(t, c):
            for k in range(TOP_K):
                _row_copy(yb_ref, dest_ref[TOP_K * (tile * tm + t) + k], y_ref.at[b, k], t, sem.at[b]).start()
            return c

        lax.fori_loop(0, tm, issue, 0, unroll=8)

    @pl.when(i == 0)
    def _():
        gather(0, 0)

    @pl.when(i + 1 < pl.num_programs(0))
    def _():
        gather(i + 1, 1 - buf)

    def drain(t, c):
        for k in range(TOP_K):
            _row_copy(yb_ref, 0, y_ref.at[buf, k], 0, sem.at[buf]).wait()
        return c

    lax.fori_loop(0, tm, drain, 0, unroll=8)
    g0 = gate_ref[:, 0:1]
    g1 = gate_ref[:, 1:2]
    y = x_ref[...] + (g0 * y_ref[buf, 0] + g1 * y_ref[buf, 1])
    o_ref[...] = _rms(y, gf_ref[...]) if final else y


def combine(dest, x, gates, yb, g_final, *, tm, final):
    n, d = x.shape
    grid_spec = pltpu.PrefetchScalarGridSpec(
        num_scalar_prefetch=1,
        grid=(n // tm,),
        in_specs=[pl.BlockSpec((tm, d), lambda i, ds: (i, 0)),
                  pl.BlockSpec((tm, LANES), lambda i, ds: (i, 0)),
                  pl.BlockSpec((1, d), lambda i, ds: (0, 0)),
                  pl.BlockSpec(memory_space=pl.ANY)],
        out_specs=pl.BlockSpec((tm, d), lambda i, ds: (i, 0)),
        scratch_shapes=[pltpu.VMEM((2, TOP_K, tm, d), F32), pltpu.SemaphoreType.DMA((2,))],
    )
    return pl.pallas_call(
        partial(_combine_kernel, tm=tm, final=final),
        grid_spec=grid_spec,
        out_shape=jax.ShapeDtypeStruct((n, d), F32),
        compiler_params=_cparams(("arbitrary",)),
        name="moe_combine",
    )(dest, x, gates, g_final.reshape(1, d), yb)


def moe(streams, routed, g, w_gate, w_up, w_down, g_final, *, tm_tok, tm_blk, tf, final):
    flat_e = jnp.concatenate([idx[:, :TOP_K].reshape(-1) for idx, _ in routed])
    n = flat_e.shape[0] // TOP_K
    d = streams[0].shape[1]
    onehot = (flat_e[:, None] == jnp.arange(N_EXPERTS, dtype=jnp.int32)[None, :]).astype(jnp.int32)
    csum = jnp.cumsum(onehot, axis=0)
    counts = csum[-1]
    padded = (counts + tm_blk - 1) // tm_blk * tm_blk
    pad_ends = jnp.cumsum(padded)
    pad_starts = pad_ends - padded
    dest = jnp.sum(onehot * (pad_starts[None, :] + csum - 1), axis=1).astype(jnp.int32)
    n_blocks = -(-(n * TOP_K + N_EXPERTS * (tm_blk - 1)) // tm_blk)
    block_start = jnp.arange(n_blocks, dtype=jnp.int32) * tm_blk
    block_expert = jnp.minimum(jnp.sum(pad_ends[None, :] <= block_start[:, None], axis=1),
                               N_EXPERTS - 1).astype(jnp.int32)
    n_used = (pad_ends[-1] // tm_blk).astype(jnp.int32).reshape(1)
    n_slots = n_blocks * tm_blk
    zero_lo = jnp.concatenate([pad_starts + counts, pad_ends[-1:]])
    zero_hi = jnp.concatenate([pad_ends, jnp.full((1,), n_slots, pad_ends.dtype)])
    pad_rows = jnp.stack([zero_lo, zero_hi], axis=1).reshape(-1).astype(jnp.int32)
    tm_tok = min([tm_tok] + [x.shape[0] for x in streams])
    slots = dispatch(dest, pad_rows, streams, g, n_slots, tm=tm_tok)
    dests, start = [], 0
    for x in streams:
        dests.append(dest[start:start + TOP_K * x.shape[0]])
        start += TOP_K * x.shape[0]
    yb = glu(slots, g, w_gate, w_up, w_down, block_expert, n_used, tm=tm_blk, tf=tf, dense=False)
    return [combine(dst, x, gates, yb, g_final, tm=min(tm_tok, x.shape[0]), final=final)
            for dst, x, (_, gates) in zip(dests, streams, routed)]


def _cast_kernel(w_ref, o_ref):
    o_ref[...] = w_ref[...].astype(o_ref.dtype)


def cast_group(w, first, count):
    _, r, c = w.shape
    tr = r
    while tr * c > CAST_BLOCK_ELEMS and tr % 32 == 0:
        tr //= 2
    return pl.pallas_call(
        _cast_kernel,
        grid=(count, r // tr),
        in_specs=[pl.BlockSpec((1, tr, c), lambda e, i: (first + e, i, 0))],
        out_specs=pl.BlockSpec((1, tr, c), lambda e, i: (e, i, 0)),
        out_shape=jax.ShapeDtypeStruct((count, r, c), BF16),
        compiler_params=_cparams(("parallel", "parallel")),
        name="cast_bf16",
    )(w)


def _final_norm_kernel(x_ref, g_ref, o_ref):
    o_ref[...] = _rms(x_ref[...], g_ref[...])


def final_norm(x, g, *, tm):
    n, d = x.shape
    return pl.pallas_call(
        _final_norm_kernel,
        grid=(n // tm,),
        in_specs=[pl.BlockSpec((tm, d), lambda i: (i, 0)), pl.BlockSpec((1, d), lambda i: (0, 0))],
        out_specs=pl.BlockSpec((tm, d), lambda i: (i, 0)),
        out_shape=jax.ShapeDtypeStruct((n, d), F32),
        compiler_params=_cparams(("parallel",)),
        name="final_norm",
    )(x, g.reshape(1, d))


def _rope_tables(pos):
    half = QK_ROPE_DIM // 2
    inv_freq = ROPE_THETA ** (-jnp.arange(half, dtype=F32) / half)
    ang = pos.astype(F32)[:, None] * inv_freq[None, :]
    cos = jnp.tile(jnp.cos(ang), (1, 2))
    sin = jnp.tile(jnp.sin(ang), (1, 2))
    z = jnp.zeros_like(cos)
    cat = lambda a, b: jnp.concatenate([a, b], axis=1)
    return (cat(cos, z), cat(z, cos), cat(sin, z), cat(z, sin), cat(cos, cos), cat(sin, sin))


def _rot_cols(w):
    half = QK_ROPE_DIM // 2
    return jnp.concatenate([-w[..., half:], w[..., :half]], axis=-1)


def _router_weights(w_router):
    w = jnp.pad(w_router, ((0, 0), (0, LANES - N_EXPERTS)))
    hi = w.astype(BF16)
    lo = (w - hi.astype(F32)).astype(BF16)
    return jnp.concatenate([hi, lo], axis=1)


def _mla_weights(w_uq, w_dkv):
    ql = w_uq.shape[0]
    wq = w_uq.reshape(ql, MLA_HEADS, QK_NOPE_DIM + QK_ROPE_DIM)
    nope, pe = wq[..., :QK_NOPE_DIM], wq[..., QK_NOPE_DIM:]
    z = jnp.zeros_like(pe)
    even = jnp.concatenate([nope, pe, z], axis=-1)
    odd = jnp.concatenate([nope, z, pe], axis=-1)
    is_even = (jnp.arange(MLA_HEADS) % 2 == 0)[None, :, None]
    w_uq_ext = jnp.where(is_even, even, odd).reshape(ql, MLA_HEADS * Q_HEAD_W).astype(BF16)
    w_uq_rot = _rot_cols(pe).reshape(ql, MLA_HEADS * QK_ROPE_DIM).astype(BF16)
    wc, wp = w_dkv[:, :KV_LORA], w_dkv[:, KV_LORA:]
    wr = _rot_cols(wp)
    w_dkv_ext = jnp.concatenate([wc, wp, wp, wr, wr], axis=1).astype(BF16)
    return w_uq_ext, w_uq_rot, w_dkv_ext


def kernel(x_prompt, x_sample, cache_l1_ckv, cache_l1_kpe, cache_l3_ckv, cache_l3_kpe, cache_mem_k, cache_mem_v, mem_prompt, g_mix, g_xattn, g_mem, g_ffn, g_final, sgu_w_in, sgu_b_in, sgu_ln_g, sgu_ln_b, sgu_w_s, sgu_b_s, sgu_w_out, mla_w_dq, mla_g_q, mla_w_uq, mla_w_dkv, mla_g_kv, mla_w_uk, mla_w_uv, mla_w_o, xa_w_q, xa_w_k, xa_w_v, xa_w_o, ffn_w_gate, ffn_w_up, ffn_w_down, moe_w_router, moe_w_gate, moe_w_up, moe_w_down):
    bp, sp, d = x_prompt.shape
    bs, ss, _ = x_sample.shape
    past = cache_l1_ckv.shape[1]
    depth = g_mix.shape[0]
    n_p, n_s = bp * sp, bs * ss
    mem_tokens = mem_prompt.shape[1]
    xa_w = XA_HEADS * XA_HEAD_DIM
    mla_cache = ((cache_l1_ckv, cache_l1_kpe), (cache_l3_ckv, cache_l3_kpe))

    bf = lambda a: a.astype(BF16)
    xp = x_prompt.reshape(n_p, d)
    xs = x_sample.reshape(n_s, d)
    tm_p, tm_s = min(ROW_TILE, n_p), min(ROW_TILE, n_s)
    tq_p = min(ROW_TILE, sp)
    tf = min(FF_TILE, ffn_w_gate.shape[2])

    mk_all, mv_all = mem_kv(mem_prompt.reshape(bp * mem_tokens, d), g_mem, bf(xa_w_k), bf(xa_w_v),
                            tm=min(ROW_TILE, bp * mem_tokens))
    new_mem_k = mk_all.reshape(depth, bp, mem_tokens, XA_HEADS, XA_HEAD_DIM)
    new_mem_v = mv_all.reshape(depth, bp, mem_tokens, XA_HEADS, XA_HEAD_DIM)

    tabs_p = _rope_tables(jnp.arange(sp, dtype=jnp.int32))
    tabs_s = _rope_tables(past + jnp.arange(ss, dtype=jnp.int32))

    ckv_p, kpe_p, ckv_s, kpe_s, sgu_v_s = [], [], [], [], []
    for layer in range(depth):
        i = layer // 2
        if layer % 2 == 0:
            w_in, w_out = bf(sgu_w_in[i]), bf(sgu_w_out[i])
            sgu_args = (w_in, sgu_b_in[i], sgu_ln_g[i], sgu_ln_b[i])
            zp, _ = sgu_in(xp, g_mix[layer], *sgu_args, tm=tm_p, emit_v=False)
            zs, v_new = sgu_in(xs, g_mix[layer], *sgu_args, tm=tm_s, emit_v=True)
            sgu_v_s.append(v_new.reshape(bs, ss, -1))
            b_col = sgu_b_s[i][:, :, None]
            xp = sgu_mix(xp, zp, sgu_w_s[i], b_col, w_out, tm=MIX_TILE, seg=SGU_CHUNK)
            rep = SGU_CHUNK // ss
            w_s_s = jnp.tile(sgu_w_s[i][:, :ss, :ss], (1, rep, rep))
            b_s_s = jnp.tile(sgu_b_s[i][:, :ss], (1, rep))[:, :, None]
            xs = sgu_mix(xs, zs, w_s_s, b_s_s, w_out, tm=MIX_TILE, seg=ss)
        else:
            w_uq_ext, w_uq_rot, w_dkv_ext = _mla_weights(mla_w_uq[i], mla_w_dkv[i])
            proj_w = (bf(mla_w_dq[i]), mla_g_q[i], w_uq_ext, w_uq_rot, w_dkv_ext, mla_g_kv[i])
            w_uk, w_uv, w_o = bf(mla_w_uk[i]), bf(mla_w_uv[i]), bf(mla_w_o[i])
            q, ckv, ckv_b, kpe = mla_proj(xp, g_mix[layer], *proj_w, tabs_p, tm=tq_p, tab_blocks=sp // tq_p)
            ckv_p.append(ckv.reshape(bp, sp, KV_LORA))
            kpe_p.append(kpe[:, :QK_ROPE_DIM].reshape(bp, sp, QK_ROPE_DIM))
            kn, vv = kv_up(ckv_b, w_uk, w_uv, tm=tm_p)
            hm = lambda a, b, s: a.reshape(MLA_HEADS, b, s, a.shape[-1])
            moe_w = [w.reshape((-1,) + w.shape[2:]) for w in (moe_w_gate, moe_w_up, moe_w_down)]
            ride = all(_cast_row_block(w.shape[1], N_EXPERTS, _attn_steps(bp, sp, tq_p)) for w in moe_w)
            cast = [(w, i * N_EXPERTS, N_EXPERTS) for w in moe_w] if ride else []
            o, moe_bf = attention(hm(q, bp, sp), hm(kn, bp, sp), bf(kpe).reshape(bp, sp, LANES), hm(vv, bp, sp),
                                  cast, t=tq_p)
            if not ride:
                moe_bf = [cast_group(w, i * N_EXPERTS, N_EXPERTS) for w in moe_w]
            xp = proj_res(xp, o.reshape(n_p, -1), w_o, tm=tm_p)
            q, ckv, ckv_b, kpe = mla_proj(xs, g_mix[layer], *proj_w, tabs_s, tm=ss, tab_blocks=1)
            ckv_s.append(ckv.reshape(bs, ss, KV_LORA))
            kpe_s.append(kpe[:, :QK_ROPE_DIM].reshape(bs, ss, QK_ROPE_DIM))
            c_ckv, c_kpe = mla_cache[i]
            kv_len = past + ss
            sk = -(-kv_len // LANES) * LANES
            pad = sk - kv_len
            ckv_all = jnp.concatenate([bf(c_ckv), ckv_b.reshape(bs, ss, KV_LORA),
                                       jnp.zeros((bs, pad, KV_LORA), BF16)], axis=1)
            kpe_all = jnp.concatenate([bf(jnp.tile(c_kpe, (1, 1, 2))), bf(kpe).reshape(bs, ss, LANES),
                                       jnp.zeros((bs, pad, LANES), BF16)], axis=1)
            o = latent_attention(hm(q, bs, ss), ckv_all, kpe_all, w_uk, w_uv, kv_len=kv_len)
            xs = proj_res(xs, o.reshape(n_s, -1), w_o, tm=tm_s)

        w_q, w_o = bf(xa_w_q[layer]), bf(xa_w_o[layer])
        is_moe = layer % 2 == 1
        router = (g_ffn[layer], _router_weights(moe_w_router[i])) if is_moe else None
        rp = xattn(xp.reshape(bp, sp, d), g_xattn[layer], w_q, bf(mk_all[layer]).reshape(bp, mem_tokens, xa_w),
                   bf(mv_all[layer]).reshape(bp, mem_tokens, xa_w), w_o, router, tm=tq_p)
        rs = xattn(xs.reshape(bs, ss, d), g_xattn[layer], w_q,
                   bf(cache_mem_k[layer]).reshape(bs, mem_tokens, xa_w),
                   bf(cache_mem_v[layer]).reshape(bs, mem_tokens, xa_w), w_o, router, tm=ss)
        xp, xs = rp[0].reshape(n_p, d), rs[0].reshape(n_s, d)

        if not is_moe:
            wg, wu, wd = (cast_group(w, i, 1) for w in (ffn_w_gate, ffn_w_up, ffn_w_down))
            ffn = lambda x, tm: glu(x, g_ffn[layer], wg, wu, wd, jnp.zeros((x.shape[0] // tm,), jnp.int32),
                                    jnp.full((1,), x.shape[0] // tm, jnp.int32), tm=tm, tf=tf, dense=True)
            xp, xs = ffn(xp, tm_p), ffn(xs, tm_s)
        else:
            wg, wu, wd = moe_bf
            routed = [(r[1].reshape(-1, LANES), r[2].reshape(-1, LANES)) for r in (rp, rs)]
            xp, xs = moe([xp, xs], routed, g_ffn[layer], wg, wu, wd, g_final, tm_tok=MOE_ROW_TILE, tm_blk=tm_p, tf=tf,
                         final=layer == depth - 1)

    if depth % 2 == 1:
        xp, xs = final_norm(xp, g_final, tm=tm_p), final_norm(xs, g_final, tm=tm_s)
    y_prompt = xp.reshape(bp, sp, d)
    y_sample = xs.reshape(bs, ss, d)
    return (y_prompt, y_sample,
            ckv_p[0], kpe_p[0], ckv_p[1], kpe_p[1],
            new_mem_k, new_mem_v,
            ckv_s[0], kpe_s[0], ckv_s[1], kpe_s[1],
            sgu_v_s[0], sgu_v_s[1])
```

```python
from functools import partial

import numpy as np
import jax
import jax.numpy as jnp
from jax import lax
from jax.experimental import pallas as pl
from jax.experimental.pallas import tpu as pltpu

F32 = jnp.float32
BF16 = jnp.bfloat16

CHUNK = 64
SGU_CHUNK = 128
SGU_GROUPS = 8
MLA_HEADS = 16
QK_NOPE_DIM = 128
QK_ROPE_DIM = 64
V_HEAD_DIM = 128
KV_LORA = 512
ROPE_THETA = 10000.0
XA_HEADS = 4
XA_HEAD_DIM = 128
N_EXPERTS = 8
TOP_K = 2
NORM_EPS = 1e-6

LANES = 128
BF16_SUBLANES = 16
Q_HEAD_W = 2 * LANES
VMEM_LIMIT = 56 * 1024 * 1024
NEG_BIG = -1e30
ROW_TILE = 512
MIX_TILE = 256
MOE_ROW_TILE = 512
FF_TILE = 1024
COL_CHUNK = 512
CAST_BLOCK_ELEMS = 1024 * 1024
HEADS_PER_ITER = 16


def _cparams(sem):
    return pltpu.CompilerParams(dimension_semantics=sem, vmem_limit_bytes=VMEM_LIMIT)


def _rms(x, g):
    return x * lax.rsqrt(jnp.mean(x * x, axis=-1, keepdims=True) + NORM_EPS) * g


def _dot(a, b):
    return jnp.dot(a, b, preferred_element_type=F32)


def _dot_nt(a, b):
    return lax.dot_general(a, b, (((1,), (1,)), ((), ())), preferred_element_type=F32)


def _col_chunks(n, width):
    return [(c, min(width, n - c)) for c in range(0, n, width)]


def _sgu_in_kernel(x_ref, g_ref, w_ref, b_ref, lng_ref, lnb_ref, *rest, emit_v, n_cast):
    w_refs, rest = rest[:n_cast], rest[n_cast:]
    z_ref, rest = rest[0], rest[1:]
    if emit_v:
        v_ref, rest = rest[0], rest[1:]
    wo_refs, (zs_ref,) = rest[:n_cast], rest[n_cast:]
    for cw_ref, co_ref in zip(w_refs, wo_refs):
        co_ref[...] = cw_ref[...].astype(co_ref.dtype)
    j = pl.program_id(0)
    h = _rms(x_ref[...], g_ref[...]).astype(BF16)
    for c, w in _col_chunks(w_ref.shape[1], COL_CHUNK):
        a = _dot(h, w_ref[:, c:c + w]) + b_ref[:, c:c + w]
        zs_ref[:, c:c + w] = 0.5 * a * (1.0 + lax.erf(a * np.float32(np.sqrt(0.5))))

    @pl.when(j == 0)
    def _():
        z_ref[...] = zs_ref[...].astype(z_ref.dtype)

    @pl.when(j == 1)
    def _():
        z = zs_ref[...]
        mu = jnp.mean(z, axis=-1, keepdims=True)
        zc = z - mu
        var = jnp.mean(zc * zc, axis=-1, keepdims=True)
        v = zc * lax.rsqrt(var + NORM_EPS) * lng_ref[...] + lnb_ref[...]
        z_ref[...] = v.astype(z_ref.dtype)
        if emit_v:
            v_ref[...] = v


def sgu_in(x, g, w_in, b_in, ln_g, ln_b, cast=(), *, tm, emit_v):
    n, d = x.shape
    ds = w_in.shape[1] // 2
    nt = n // tm
    out_shape = [jax.ShapeDtypeStruct((n, 2 * ds), BF16)]
    out_specs = [pl.BlockSpec((tm, ds), lambda j, i: (i, j))]
    if emit_v:
        out_shape.append(jax.ShapeDtypeStruct((n, ds), F32))
        out_specs.append(pl.BlockSpec((tm, ds), lambda j, i: (i * j, 0)))
    w_args, w_in_specs = [], []
    for w, first, count in cast:
        _, r, c = w.shape
        rb = _cast_row_block(r, count, 2 * nt)
        nblk = count * r // rb
        w_args.append(w.reshape(-1, c))
        w_in_specs.append(pl.BlockSpec(
            (rb, c), lambda j, i, o=first * r // rb, nb=nblk: (o + jnp.minimum(j * nt + i, nb - 1), 0)))
        out_specs.append(pl.BlockSpec((rb, c), lambda j, i, nb=nblk: (jnp.minimum(j * nt + i, nb - 1), 0)))
        out_shape.append(jax.ShapeDtypeStruct((count * r, c), BF16))
    res = pl.pallas_call(
        partial(_sgu_in_kernel, emit_v=emit_v, n_cast=len(cast)),
        grid=(2, nt),
        in_specs=[
            pl.BlockSpec((tm, d), lambda j, i: (i, 0)),
            pl.BlockSpec((1, d), lambda j, i: (0, 0)),
            pl.BlockSpec((d, ds), lambda j, i: (0, j)),
            pl.BlockSpec((1, ds), lambda j, i: (0, j)),
            pl.BlockSpec((1, ds), lambda j, i: (0, 0)),
            pl.BlockSpec((1, ds), lambda j, i: (0, 0)),
        ] + w_in_specs,
        out_specs=out_specs,
        out_shape=out_shape,
        scratch_shapes=[pltpu.VMEM((tm, ds), F32)],
        compiler_params=_cparams(("arbitrary", "arbitrary")),
        name="sgu_in",
    )(x, g.reshape(1, d), w_in, b_in.reshape(1, -1), ln_g.reshape(1, ds), ln_b.reshape(1, ds), *w_args)
    n_main = 2 if emit_v else 1
    casted = [o.reshape(count, w.shape[1], w.shape[2]) for o, (w, _, count) in zip(res[n_main:], cast)]
    return res[0], (res[1] if emit_v else None), casted


def _sgu_mix_kernel(x_ref, u_ref, v_ref, ws_ref, bs_ref, wo_ref, o_ref, gs_ref, *, seg):
    tm = x_ref.shape[0]
    gd = v_ref.shape[1] // SGU_GROUPS
    row = lax.broadcasted_iota(jnp.int32, (SGU_CHUNK, SGU_CHUNK), 0)
    col = lax.broadcasted_iota(jnp.int32, (SGU_CHUNK, SGU_CHUNK), 1)
    keep = (col <= row) & ((col // seg) == (row // seg))
    for g in range(SGU_GROUPS):
        wg = jnp.where(keep, ws_ref[g], 0.0).astype(BF16)
        bg = bs_ref[g]
        for c in range(tm // SGU_CHUNK):
            rs = slice(c * SGU_CHUNK, (c + 1) * SGU_CHUNK)
            cs = slice(g * gd, (g + 1) * gd)
            mixed = _dot(wg, v_ref[rs, cs]) + bg
            gs_ref[rs, cs] = (u_ref[rs, cs].astype(F32) * mixed).astype(BF16)
    for c, w in _col_chunks(wo_ref.shape[1], COL_CHUNK):
        o_ref[:, c:c + w] = x_ref[:, c:c + w] + _dot(gs_ref[...], wo_ref[:, c:c + w])


def sgu_mix(x, z, w_s, b_s, w_out, *, tm, seg):
    n, d = x.shape
    ds = z.shape[1] // 2
    return pl.pallas_call(
        partial(_sgu_mix_kernel, seg=seg),
        grid=(n // tm,),
        in_specs=[
            pl.BlockSpec((tm, d), lambda i: (i, 0)),
            pl.BlockSpec((tm, ds), lambda i: (i, 0)),
            pl.BlockSpec((tm, ds), lambda i: (i, 1)),
            pl.BlockSpec(w_s.shape, lambda i: (0, 0, 0)),
            pl.BlockSpec(b_s.shape, lambda i: (0, 0, 0)),
            pl.BlockSpec(w_out.shape, lambda i: (0, 0)),
        ],
        out_specs=pl.BlockSpec((tm, d), lambda i: (i, 0)),
        out_shape=jax.ShapeDtypeStruct((n, d), F32),
        scratch_shapes=[pltpu.VMEM((tm, ds), BF16)],
        compiler_params=_cparams(("parallel",)),
        name="sgu_mix",
    )(x, z, z, w_s, b_s, w_out)


def _mem_kv_kernel(m_ref, g_ref, wk_ref, wv_ref, k_ref, v_ref):
    h = _rms(m_ref[...], g_ref[0]).astype(BF16)
    k_ref[0] = _dot(h, wk_ref[0])
    v_ref[0] = _dot(h, wv_ref[0])


def mem_kv(mem, g_mem, w_k, w_v, *, tm):
    r, d = mem.shape
    nl, _, w = w_k.shape
    return pl.pallas_call(
        _mem_kv_kernel,
        grid=(nl, r // tm),
        in_specs=[
            pl.BlockSpec((tm, d), lambda l, i: (i, 0)),
            pl.BlockSpec((1, 1, d), lambda l, i: (l, 0, 0)),
            pl.BlockSpec((1, d, w), lambda l, i: (l, 0, 0)),
            pl.BlockSpec((1, d, w), lambda l, i: (l, 0, 0)),
        ],
        out_specs=[pl.BlockSpec((1, tm, w), lambda l, i: (l, i, 0))] * 2,
        out_shape=[jax.ShapeDtypeStruct((nl, r, w), F32)] * 2,
        compiler_params=_cparams(("parallel", "parallel")),
        name="mem_kv",
    )(mem, g_mem.reshape(nl, 1, d), w_k, w_v)


def _top2_route(h, w_hl):
    h_hi = h.astype(BF16)
    h_lo = (h - h_hi.astype(F32)).astype(BF16)
    a = _dot(h_hi, w_hl[...])
    logits = a[:, :LANES] + a[:, LANES:] + _dot(h_lo, w_hl[:, :LANES])
    lane_i = lax.broadcasted_iota(jnp.int32, logits.shape, 1)
    lane = lane_i.astype(F32)
    logits = jnp.where(lane_i < N_EXPERTS, logits, -jnp.inf)
    m1 = jnp.max(logits, axis=-1, keepdims=True)
    i1 = jnp.min(jnp.where(logits == m1, lane, float(LANES)), axis=-1, keepdims=True)
    rest = jnp.where(lane == i1, -jnp.inf, logits)
    m2 = jnp.max(rest, axis=-1, keepdims=True)
    i2 = jnp.min(jnp.where(rest == m2, lane, float(LANES)), axis=-1, keepdims=True)
    e2 = jnp.exp(m2 - m1)
    g1 = 1.0 / (1.0 + e2)
    g2 = e2 / (1.0 + e2)
    idx = jnp.where(lane_i == 0, i1, jnp.where(lane_i == 1, i2, 0.0)).astype(jnp.int32)
    gates = jnp.where(lane_i == 0, g1, jnp.where(lane_i == 1, g2, 0.0))
    return idx, gates


def _xattn_kernel(x_ref, g_ref, wq_ref, k_ref, v_ref, wo_ref, *rest, route):
    if route:
        gf_ref, wr_ref, o_ref, idx_ref, gate_ref, os_ref = rest
    else:
        o_ref, os_ref = rest
    x = x_ref[0]
    h = _rms(x, g_ref[...]).astype(BF16)
    q = (_dot(h, wq_ref[...]) * np.float32(XA_HEAD_DIM ** -0.5)).astype(BF16)
    for hd in range(XA_HEADS):
        cs = slice(hd * XA_HEAD_DIM, (hd + 1) * XA_HEAD_DIM)
        s = _dot_nt(q[:, cs], k_ref[0, :, cs])
        p = jnp.exp(s - jnp.max(s, axis=-1, keepdims=True))
        l = jnp.sum(p, axis=-1, keepdims=True)
        os_ref[:, cs] = (_dot(p.astype(BF16), v_ref[0, :, cs]) / l).astype(BF16)
    for c, w in _col_chunks(wo_ref.shape[1], COL_CHUNK):
        o_ref[0, :, c:c + w] = x[:, c:c + w] + _dot(os_ref[...], wo_ref[:, c:c + w])
    if route:
        idx_ref[0], gate_ref[0] = _top2_route(_rms(o_ref[0], gf_ref[...]), wr_ref)


def xattn(x, g, w_q, mem_k, mem_v, w_o, router=None, *, tm):
    b, s, d = x.shape
    m, xw = mem_k.shape[1:]
    row_spec = lambda w: pl.BlockSpec((1, tm, w), lambda bi, i: (bi, i, 0))
    in_specs = [
        row_spec(d),
        pl.BlockSpec((1, d), lambda bi, i: (0, 0)),
        pl.BlockSpec((d, xw), lambda bi, i: (0, 0)),
        pl.BlockSpec((1, m, xw), lambda bi, i: (bi, 0, 0)),
        pl.BlockSpec((1, m, xw), lambda bi, i: (bi, 0, 0)),
        pl.BlockSpec((xw, d), lambda bi, i: (0, 0)),
    ]
    args = [x, g.reshape(1, d), w_q, mem_k, mem_v, w_o]
    out_specs, out_shape = [row_spec(d)], [jax.ShapeDtypeStruct((b, s, d), F32)]
    if router is not None:
        in_specs += [pl.BlockSpec((1, d), lambda bi, i: (0, 0)), pl.BlockSpec((d, 2 * LANES), lambda bi, i: (0, 0))]
        args += [router[0].reshape(1, d), router[1]]
        out_specs += [row_spec(LANES), row_spec(LANES)]
        out_shape += [jax.ShapeDtypeStruct((b, s, LANES), jnp.int32), jax.ShapeDtypeStruct((b, s, LANES), F32)]
    return pl.pallas_call(
        partial(_xattn_kernel, route=router is not None),
        grid=(b, s // tm),
        in_specs=in_specs,
        out_specs=out_specs,
        out_shape=out_shape,
        scratch_shapes=[pltpu.VMEM((tm, xw), BF16)],
        compiler_params=_cparams(("parallel", "parallel")),
        name="xattn",
    )(*args)


def _glu_kernel(be_ref, nb_ref, x_ref, g_ref, wg_ref, wu_ref, wd_ref, o_ref, h_ref, *, dense):
    i = pl.program_id(0)
    f = pl.program_id(1)

    @pl.when(i < nb_ref[0])
    def _():
        @pl.when(f == 0)
        def _():
            if dense:
                x = x_ref[...]
                h_ref[...] = _rms(x, g_ref[...]).astype(BF16)
                o_ref[...] = x
            else:
                h_ref[...] = x_ref[...].astype(BF16)
                o_ref[...] = jnp.zeros_like(o_ref)

        h = h_ref[...]
        a = _dot(h, wg_ref[0])
        b = _dot(h, wu_ref[0])
        hid = (a * jax.nn.sigmoid(a) * b).astype(BF16)
        o_ref[...] += _dot(hid, wd_ref[0])

    @pl.when(jnp.logical_and(i >= nb_ref[0], f == 0))
    def _():
        o_ref[...] = jnp.zeros_like(o_ref)


def glu(x, g, w_gate, w_up, w_down, block_expert, n_blocks_used, *, tm, tf, dense):
    r, d = x.shape
    ff = w_gate.shape[2]
    nf = ff // tf

    def wsel(i, f, be, nb):
        live = i < nb[0]
        return be[i], jnp.where(live, f, nf - 1)

    grid_spec = pltpu.PrefetchScalarGridSpec(
        num_scalar_prefetch=2,
        grid=(r // tm, nf),
        in_specs=[
            pl.BlockSpec((tm, d), lambda i, f, be, nb: (jnp.minimum(i, nb[0] - 1), 0)),
            pl.BlockSpec((1, d), lambda i, f, be, nb: (0, 0)),
            pl.BlockSpec((1, d, tf), lambda i, f, be, nb: (wsel(i, f, be, nb)[0], 0, wsel(i, f, be, nb)[1])),
            pl.BlockSpec((1, d, tf), lambda i, f, be, nb: (wsel(i, f, be, nb)[0], 0, wsel(i, f, be, nb)[1])),
            pl.BlockSpec((1, tf, d), lambda i, f, be, nb: (wsel(i, f, be, nb)[0], wsel(i, f, be, nb)[1], 0)),
        ],
        out_specs=pl.BlockSpec((tm, d), lambda i, f, be, nb: (i, 0)),
        scratch_shapes=[pltpu.VMEM((tm, d), BF16)],
    )
    return pl.pallas_call(
        partial(_glu_kernel, dense=dense),
        grid_spec=grid_spec,
        out_shape=jax.ShapeDtypeStruct((r, d), F32),
        compiler_params=_cparams(("arbitrary", "arbitrary")),
        name="glu_dense" if dense else "glu_expert",
    )(block_expert, n_blocks_used, x, g.reshape(1, d), w_gate, w_up, w_down)


def _mla_proj_kernel(x_ref, g_ref, wdq_ref, gq_ref, wuq_ref, wrot_ref, wkv_ref, gkv_ref,
                     ce_ref, co_ref, se_ref, so_ref, ck_ref, sk_ref,
                     q_ref, ckv_ref, ckvb_ref, kpe_ref, cq_ref):
    h = _rms(x_ref[...], g_ref[...]).astype(BF16)
    cq_ref[...] = _rms(_dot(h, wdq_ref[...]), gq_ref[...]).astype(BF16)
    kv = _dot(h, wkv_ref[...])
    ckv = _rms(kv[:, :KV_LORA], gkv_ref[...])
    ckv_ref[...] = ckv
    ckvb_ref[...] = ckv.astype(BF16)
    kpe_ref[...] = (kv[:, KV_LORA:KV_LORA + LANES] * ck_ref[...]
                    + kv[:, KV_LORA + LANES:KV_LORA + 2 * LANES] * sk_ref[...])
    scale = np.float32((QK_NOPE_DIM + QK_ROPE_DIM) ** -0.5 * np.log2(np.e))
    cq = cq_ref[...]
    for p in range(MLA_HEADS // 2):
        a = _dot(cq, wuq_ref[:, 2 * p * Q_HEAD_W:(2 * p + 2) * Q_HEAD_W])
        r = _dot(cq, wrot_ref[:, p * LANES:(p + 1) * LANES])
        for k, (c_ref, s_ref) in enumerate(((ce_ref, se_ref), (co_ref, so_ref))):
            q_ref[2 * p + k, :, :LANES] = (a[:, k * Q_HEAD_W:k * Q_HEAD_W + LANES] * scale).astype(BF16)
            pe = a[:, k * Q_HEAD_W + LANES:(k + 1) * Q_HEAD_W] * c_ref[...] + r * s_ref[...]
            q_ref[2 * p + k, :, LANES:] = (pe * scale).astype(BF16)


def mla_proj(x, g, w_dq, g_q, w_uq_ext, w_uq_rot, w_dkv_ext, g_kv, tabs, *, tm, tab_blocks):
    n, d = x.shape
    ql = w_dq.shape[1]
    full = lambda a: pl.BlockSpec(a.shape, lambda i: (0,) * a.ndim)
    tab_spec = pl.BlockSpec((tm, LANES), lambda i: (i % tab_blocks, 0))
    return pl.pallas_call(
        _mla_proj_kernel,
        grid=(n // tm,),
        in_specs=[pl.BlockSpec((tm, d), lambda i: (i, 0)), pl.BlockSpec((1, d), lambda i: (0, 0)),
                  full(w_dq), pl.BlockSpec((1, ql), lambda i: (0, 0)), full(w_uq_ext), full(w_uq_rot),
                  full(w_dkv_ext), pl.BlockSpec((1, KV_LORA), lambda i: (0, 0))] + [tab_spec] * 6,
        out_specs=[pl.BlockSpec((MLA_HEADS, tm, Q_HEAD_W), lambda i: (0, i, 0)),
                   pl.BlockSpec((tm, KV_LORA), lambda i: (i, 0)),
                   pl.BlockSpec((tm, KV_LORA), lambda i: (i, 0)),
                   pl.BlockSpec((tm, LANES), lambda i: (i, 0))],
        out_shape=[jax.ShapeDtypeStruct((MLA_HEADS, n, Q_HEAD_W), BF16),
                   jax.ShapeDtypeStruct((n, KV_LORA), F32),
                   jax.ShapeDtypeStruct((n, KV_LORA), BF16),
                   jax.ShapeDtypeStruct((n, LANES), F32)],
        scratch_shapes=[pltpu.VMEM((tm, ql), BF16)],
        compiler_params=_cparams(("parallel",)),
        name="mla_proj",
    )(x, g.reshape(1, d), w_dq, g_q.reshape(1, ql), w_uq_ext, w_uq_rot, w_dkv_ext,
      g_kv.reshape(1, KV_LORA), *tabs)


def _kv_up_kernel(c_ref, wk_ref, wv_ref, k_ref, v_ref):
    c = c_ref[...]
    for w_ref, o_ref in ((wk_ref, k_ref), (wv_ref, v_ref)):
        for p in range(MLA_HEADS // 2):
            r = _dot(c, w_ref[:, 2 * p * LANES:(2 * p + 2) * LANES]).astype(BF16)
            o_ref[2 * p] = r[:, :LANES]
            o_ref[2 * p + 1] = r[:, LANES:]


def kv_up(ckv, w_uk, w_uv, *, tm):
    n, c = ckv.shape
    w = w_uk.shape[1]
    return pl.pallas_call(
        _kv_up_kernel,
        grid=(n // tm,),
        in_specs=[pl.BlockSpec((tm, c), lambda i: (i, 0)),
                  pl.BlockSpec((c, w), lambda i: (0, 0)),
                  pl.BlockSpec((c, w), lambda i: (0, 0))],
        out_specs=[pl.BlockSpec((MLA_HEADS, tm, LANES), lambda i: (0, i, 0))] * 2,
        out_shape=[jax.ShapeDtypeStruct((MLA_HEADS, n, LANES), BF16)] * 2,
        compiler_params=_cparams(("parallel",)),
        name="kv_up",
    )(ckv, w_uk, w_uv)


def _attn_kernel(qi_ref, ki_ref, q_ref, kn_ref, kp_ref, v_ref, *rest, t, n_cast):
    w_refs = rest[:n_cast]
    o_ref = rest[n_cast]
    wo_refs = rest[n_cast + 1:2 * n_cast + 1]
    m_ref, l_ref, acc_ref = rest[2 * n_cast + 1:]
    for w_ref, wo_ref in zip(w_refs, wo_refs):
        wo_ref[...] = w_ref[...].astype(wo_ref.dtype)
    tq = tk = t
    step_id = pl.program_id(1)
    qi = qi_ref[step_id]
    ki = ki_ref[step_id]

    @pl.when(ki == 0)
    def _():
        m_ref[...] = jnp.full_like(m_ref, NEG_BIG)
        l_ref[...] = jnp.zeros_like(l_ref)
        acc_ref[...] = jnp.zeros_like(acc_ref)

    def run(masked):
        kp = kp_ref[0]
        ones = jnp.ones((tk, LANES), BF16)
        keep = None
        if masked:
            qc = lax.broadcasted_iota(jnp.int32, (tq, tk), 0) // CHUNK
            kc = lax.broadcasted_iota(jnp.int32, (tq, tk), 1) // CHUNK
            keep = kc <= qc

        def head(h):
            k = jnp.concatenate([kn_ref[h, 0], kp], axis=1)
            s = _dot_nt(q_ref[h, 0], k)
            if keep is not None:
                s = jnp.where(keep, s, NEG_BIG)
            m_old = m_ref[h]
            m_new = jnp.maximum(m_old, jnp.max(s, axis=-1, keepdims=True))
            alpha = jnp.exp2(m_old - m_new)
            p = jnp.exp2(s - jnp.tile(m_new, (1, tk // LANES))).astype(BF16)
            pv = _dot(p, jnp.concatenate([v_ref[h, 0], ones], axis=1))
            l_ref[h] = alpha * l_ref[h] + pv[:, V_HEAD_DIM:]
            acc_ref[h] = alpha * acc_ref[h] + pv[:, :V_HEAD_DIM]
            m_ref[h] = m_new

        def head_group(j, c):
            for k in range(HEADS_PER_ITER):
                head(HEADS_PER_ITER * j + k)
            return c

        lax.fori_loop(0, MLA_HEADS // HEADS_PER_ITER, head_group, 0)

    pl.when(ki < qi)(lambda: run(False))
    pl.when(ki == qi)(lambda: run(True))

    @pl.when(ki == qi)
    def _():
        for h in range(MLA_HEADS):
            o_ref[0, :, h * V_HEAD_DIM:(h + 1) * V_HEAD_DIM] = (acc_ref[h] / l_ref[h]).astype(o_ref.dtype)


def _cast_row_block(r, count, n_steps):
    rb = BF16_SUBLANES
    while count * r // rb > n_steps:
        rb *= 2
    return rb if r % rb == 0 else None


def _attn_steps(b, s, t):
    return b * (s // t) * (s // t + 1) // 2


def attention(q, kn, kp, v, cast, *, t):
    _, b, s, _ = q.shape
    assert s % t == 0 and t % CHUNK == 0
    pairs = [(qi, ki) for qi in range(s // t) for ki in range(qi + 1)]
    qi_tab = jnp.asarray(np.array([p[0] for p in pairs], np.int32))
    ki_tab = jnp.asarray(np.array([p[1] for p in pairs], np.int32))
    n_steps = b * len(pairs)
    step = lambda bi, s: bi * len(pairs) + s
    w_args, w_in_specs, w_out_specs, w_out_shapes = [], [], [], []
    for w, first, count in cast:
        _, r, c = w.shape
        rows = count * r
        rb = _cast_row_block(r, count, n_steps)
        nblk = rows // rb
        w_args.append(w.reshape(-1, c))
        w_in_specs.append(pl.BlockSpec(
            (rb, c), lambda bi, s, qt, kt, o=first * r // rb, n=nblk: (o + jnp.minimum(step(bi, s), n - 1), 0)))
        w_out_specs.append(pl.BlockSpec((rb, c), lambda bi, s, qt, kt, n=nblk: (jnp.minimum(step(bi, s), n - 1), 0)))
        w_out_shapes.append(jax.ShapeDtypeStruct((rows, c), BF16))
    grid_spec = pltpu.PrefetchScalarGridSpec(
        num_scalar_prefetch=2,
        grid=(b, len(pairs)),
        in_specs=[
            pl.BlockSpec((MLA_HEADS, 1, t, Q_HEAD_W), lambda bi, s, qt, kt: (0, bi, qt[s], 0)),
            pl.BlockSpec((MLA_HEADS, 1, t, QK_NOPE_DIM), lambda bi, s, qt, kt: (0, bi, kt[s], 0)),
            pl.BlockSpec((1, t, LANES), lambda bi, s, qt, kt: (bi, kt[s], 0)),
            pl.BlockSpec((MLA_HEADS, 1, t, V_HEAD_DIM), lambda bi, s, qt, kt: (0, bi, kt[s], 0)),
        ] + w_in_specs,
        out_specs=[pl.BlockSpec((1, t, MLA_HEADS * V_HEAD_DIM), lambda bi, s, qt, kt: (bi, qt[s], 0))] + w_out_specs,
        scratch_shapes=[pltpu.VMEM((MLA_HEADS, t, LANES), F32), pltpu.VMEM((MLA_HEADS, t, LANES), F32),
                        pltpu.VMEM((MLA_HEADS, t, V_HEAD_DIM), F32)],
    )
    res = pl.pallas_call(
        partial(_attn_kernel, t=t, n_cast=len(cast)),
        grid_spec=grid_spec,
        out_shape=[jax.ShapeDtypeStruct((b, s, MLA_HEADS * V_HEAD_DIM), BF16)] + w_out_shapes,
        compiler_params=_cparams(("arbitrary", "arbitrary")),
        name="attention",
    )(qi_tab, ki_tab, q, kn, kp, v, *w_args)
    return res[0], [o.reshape(count, w.shape[1], w.shape[2]) for o, (w, _, count) in zip(res[1:], cast)]


def _latent_attn_kernel(q_ref, c_ref, kp_ref, wk_ref, wv_ref, o_ref, qs_ref, *, kv_len, heads_per_chunk):
    sq = q_ref.shape[2]
    sk = c_ref.shape[1]
    for h in range(MLA_HEADS):
        rows = slice(h * sq, (h + 1) * sq)
        qa = _dot_nt(q_ref[h, 0, :, :QK_NOPE_DIM], wk_ref[:, h * QK_NOPE_DIM:(h + 1) * QK_NOPE_DIM])
        qs_ref[rows, :KV_LORA] = qa.astype(BF16)
        qs_ref[rows, KV_LORA:] = q_ref[h, 0, :, QK_NOPE_DIM:]
    c = c_ref[0]
    keys = jnp.concatenate([c, kp_ref[0]], axis=1)
    rows_per_chunk = heads_per_chunk * sq
    keep = lax.broadcasted_iota(jnp.int32, (rows_per_chunk, sk), 1) < kv_len
    for j in range(MLA_HEADS // heads_per_chunk):
        s = _dot_nt(qs_ref[j * rows_per_chunk:(j + 1) * rows_per_chunk, :], keys)
        s = jnp.where(keep, s, NEG_BIG)
        p = jnp.exp2(s - jnp.max(s, axis=-1, keepdims=True))
        l = jnp.sum(p, axis=-1, keepdims=True)
        lat = (_dot(p.astype(BF16), c) / l).astype(BF16)
        for k in range(heads_per_chunk):
            h = j * heads_per_chunk + k
            o_ref[0, :, h * V_HEAD_DIM:(h + 1) * V_HEAD_DIM] = _dot(
                lat[k * sq:(k + 1) * sq], wv_ref[:, h * V_HEAD_DIM:(h + 1) * V_HEAD_DIM]).astype(o_ref.dtype)


def latent_attention(q, ckv, kp, w_uk, w_uv, *, kv_len):
    _, b, sq, _ = q.shape
    sk = ckv.shape[1]
    return pl.pallas_call(
        partial(_latent_attn_kernel, kv_len=kv_len, heads_per_chunk=4),
        grid=(b,),
        in_specs=[pl.BlockSpec((MLA_HEADS, 1, sq, Q_HEAD_W), lambda bi: (0, bi, 0, 0)),
                  pl.BlockSpec((1, sk, KV_LORA), lambda bi: (bi, 0, 0)),
                  pl.BlockSpec((1, sk, LANES), lambda bi: (bi, 0, 0)),
                  pl.BlockSpec(w_uk.shape, lambda bi: (0, 0)),
                  pl.BlockSpec(w_uv.shape, lambda bi: (0, 0))],
        out_specs=pl.BlockSpec((1, sq, MLA_HEADS * V_HEAD_DIM), lambda bi: (bi, 0, 0)),
        out_shape=jax.ShapeDtypeStruct((b, sq, MLA_HEADS * V_HEAD_DIM), BF16),
        scratch_shapes=[pltpu.VMEM((MLA_HEADS * sq, KV_LORA + LANES), BF16)],
        compiler_params=_cparams(("parallel",)),
        name="latent_attention",
    )(q, ckv, kp, w_uk, w_uv)


def _proj_res_kernel(x_ref, a_ref, w_ref, o_ref):
    for c, w in _col_chunks(w_ref.shape[1], COL_CHUNK):
        o_ref[:, c:c + w] = x_ref[:, c:c + w] + _dot(a_ref[...], w_ref[:, c:c + w])


def proj_res(x, a, w, *, tm):
    n, d = x.shape
    k = a.shape[1]
    return pl.pallas_call(
        _proj_res_kernel,
        grid=(n // tm,),
        in_specs=[pl.BlockSpec((tm, d), lambda i: (i, 0)),
                  pl.BlockSpec((tm, k), lambda i: (i, 0)),
                  pl.BlockSpec((k, d), lambda i: (0, 0))],
        out_specs=pl.BlockSpec((tm, d), lambda i: (i, 0)),
        out_shape=jax.ShapeDtypeStruct((n, d), F32),
        compiler_params=_cparams(("parallel",)),
        name="proj_res",
    )(x, a, w)


def _row_copy(src_ref, src_row, dst_ref, dst_row, sem):
    return pltpu.make_async_copy(src_ref.at[pl.ds(src_row, 1)], dst_ref.at[pl.ds(dst_row, 1)], sem)


def _dispatch_kernel(dest_ref, pad_ref, *rest, tm, tiles):
    x_refs = rest[:len(tiles)]
    g_ref, xs_ref, h_ref, z_ref, sem, zsem = rest[len(tiles):]
    i = pl.program_id(0)
    base = i * tm

    @pl.when(i == 0)
    def _():
        z_ref[...] = jnp.zeros_like(z_ref)
        for r in range(N_EXPERTS + 1):
            lo, hi = pad_ref[2 * r], pad_ref[2 * r + 1]

            def zissue(row, c):
                _row_copy(z_ref, 0, xs_ref, row, zsem).start()
                return c

            def zdrain(row, c):
                _row_copy(z_ref, 0, xs_ref, 0, zsem).wait()
                return c

            lax.fori_loop(lo, hi, zissue, 0)
            lax.fori_loop(lo, hi, zdrain, 0)

    buf = i % 2
    first = 0
    for x_ref, n_tiles in zip(x_refs, tiles):
        @pl.when(jnp.logical_and(i >= first, i < first + n_tiles))
        def _(x_ref=x_ref):
            h_ref[buf] = _rms(x_ref[...], g_ref[...])
        first += n_tiles

    def issue(t, c):
        for k in range(TOP_K):
            _row_copy(h_ref.at[buf], t, xs_ref, dest_ref[TOP_K * (base + t) + k], sem.at[buf]).start()
        return c

    lax.fori_loop(0, tm, issue, 0, unroll=8)

    def drain(b):
        def body(t, c):
            for k in range(TOP_K):
                _row_copy(h_ref.at[b], 0, xs_ref, 0, sem.at[b]).wait()
            return c

        lax.fori_loop(0, tm, body, 0, unroll=8)

    pl.when(i > 0)(lambda: drain(1 - buf))
    pl.when(i == pl.num_programs(0) - 1)(lambda: drain(buf))


def dispatch(dest, pad_rows, streams, g, n_slots, *, tm):
    d = streams[0].shape[1]
    tiles = [x.shape[0] // tm for x in streams]
    in_specs, first = [], 0
    for n_tiles in tiles:
        in_specs.append(pl.BlockSpec(
            (tm, d), lambda i, ds, pr, first=first, n_tiles=n_tiles: (jnp.clip(i - first, 0, n_tiles - 1), 0)))
        first += n_tiles
    in_specs.append(pl.BlockSpec((1, d), lambda i, ds, pr: (0, 0)))
    grid_spec = pltpu.PrefetchScalarGridSpec(
        num_scalar_prefetch=2,
        grid=(sum(tiles),),
        in_specs=in_specs,
        out_specs=pl.BlockSpec(memory_space=pl.ANY),
        scratch_shapes=[pltpu.VMEM((2, tm, d), F32), pltpu.VMEM((8, d), F32),
                        pltpu.SemaphoreType.DMA((2,)), pltpu.SemaphoreType.DMA],
    )
    return pl.pallas_call(
        partial(_dispatch_kernel, tm=tm, tiles=tuple(tiles)),
        grid_spec=grid_spec,
        out_shape=jax.ShapeDtypeStruct((n_slots, d), F32),
        compiler_params=_cparams(("arbitrary",)),
        name="moe_dispatch",
    )(dest, pad_rows, *streams, g.reshape(1, d))


def _combine_kernel(dest_ref, x_ref, gate_ref, gf_ref, yb_ref, o_ref, y_ref, sem, *, tm, final):
    i = pl.program_id(0)
    buf = i % 2

    def gather(tile, b):
        def issue(t, c):
            for k in range(TOP_K):
                _row_copy(yb_ref, dest_ref[TOP_K * (tile * tm + t) + k], y_ref.at[b, k], t, sem.at[b]).start()
            return c

        lax.fori_loop(0, tm, issue, 0, unroll=8)

    @pl.when(i == 0)
    def _():
        gather(0, 0)

    @pl.when(i + 1 < pl.num_programs(0))
    def _():
        gather(i + 1, 1 - buf)

    def drain(t, c):
        for k in range(TOP_K):
            _row_copy(yb_ref, 0, y_ref.at[buf, k], 0, sem.at[buf]).wait()
        return c

    lax.fori_loop(0, tm, drain, 0, unroll=8)
    g0 = gate_ref[:, 0:1]
    g1 = gate_ref[:, 1:2]
    y = x_ref[...] + (g0 * y_ref[buf, 0] + g1 * y_ref[buf, 1])
    o_ref[...] = _rms(y, gf_ref[...]) if final else y


def combine(dest, x, gates, yb, g_final, *, tm, final):
    n, d = x.shape
    grid_spec = pltpu.PrefetchScalarGridSpec(
        num_scalar_prefetch=1,
        grid=(n // tm,),
        in_specs=[pl.BlockSpec((tm, d), lambda i, ds: (i, 0)),
                  pl.BlockSpec((tm, LANES), lambda i, ds: (i, 0)),
                  pl.BlockSpec((1, d), lambda i, ds: (0, 0)),
                  pl.BlockSpec(memory_space=pl.ANY)],
        out_specs=pl.BlockSpec((tm, d), lambda i, ds: (i, 0)),
        scratch_shapes=[pltpu.VMEM((2, TOP_K, tm, d), F32), pltpu.SemaphoreType.DMA((2,))],
    )
    return pl.pallas_call(
        partial(_combine_kernel, tm=tm, final=final),
        grid_spec=grid_spec,
        out_shape=jax.ShapeDtypeStruct((n, d), F32),
        compiler_params=_cparams(("arbitrary",)),
        name="moe_combine",
    )(dest, x, gates, g_final.reshape(1, d), yb)


def moe(streams, routed, g, w_gate, w_up, w_down, g_final, *, tm_tok, tm_blk, tf, final):
    flat_e = jnp.concatenate([idx[:, :TOP_K].reshape(-1) for idx, _ in routed])
    n = flat_e.shape[0] // TOP_K
    d = streams[0].shape[1]
    onehot = (flat_e[:, None] == jnp.arange(N_EXPERTS, dtype=jnp.int32)[None, :]).astype(jnp.int32)
    csum = jnp.cumsum(onehot, axis=0)
    counts = csum[-1]
    padded = (counts + tm_blk - 1) // tm_blk * tm_blk
    pad_ends = jnp.cumsum(padded)
    pad_starts = pad_ends - padded
    dest = jnp.sum(onehot * (pad_starts[None, :] + csum - 1), axis=1).astype(jnp.int32)
    n_blocks = -(-(n * TOP_K + N_EXPERTS * (tm_blk - 1)) // tm_blk)
    block_start = jnp.arange(n_blocks, dtype=jnp.int32) * tm_blk
    block_expert = jnp.minimum(jnp.sum(pad_ends[None, :] <= block_start[:, None], axis=1),
                               N_EXPERTS - 1).astype(jnp.int32)
    n_used = (pad_ends[-1] // tm_blk).astype(jnp.int32).reshape(1)
    n_slots = n_blocks * tm_blk
    zero_lo = jnp.concatenate([pad_starts + counts, pad_ends[-1:]])
    zero_hi = jnp.concatenate([pad_ends, jnp.full((1,), n_slots, pad_ends.dtype)])
    pad_rows = jnp.stack([zero_lo, zero_hi], axis=1).reshape(-1).astype(jnp.int32)
    tm_tok = min([tm_tok] + [x.shape[0] for x in streams])
    slots = dispatch(dest, pad_rows, streams, g, n_slots, tm=tm_tok)
    dests, start = [], 0
    for x in streams:
        dests.append(dest[start:start + TOP_K * x.shape[0]])
        start += TOP_K * x.shape[0]
    yb = glu(slots, g, w_gate, w_up, w_down, block_expert, n_used, tm=tm_blk, tf=tf, dense=False)
    return [combine(dst, x, gates, yb, g_final, tm=min(tm_tok, x.shape[0]), final=final)
            for dst, x, (_, gates) in zip(dests, streams, routed)]


def _cast_kernel(w_ref, o_ref):
    o_ref[...] = w_ref[...].astype(o_ref.dtype)


def cast_group(w, first, count):
    _, r, c = w.shape
    tr = r
    while tr * c > CAST_BLOCK_ELEMS and tr % 32 == 0:
        tr //= 2
    return pl.pallas_call(
        _cast_kernel,
        grid=(count, r // tr),
        in_specs=[pl.BlockSpec((1, tr, c), lambda e, i: (first + e, i, 0))],
        out_specs=pl.BlockSpec((1, tr, c), lambda e, i: (e, i, 0)),
        out_shape=jax.ShapeDtypeStruct((count, r, c), BF16),
        compiler_params=_cparams(("parallel", "parallel")),
        name="cast_bf16",
    )(w)


def _final_norm_kernel(x_ref, g_ref, o_ref):
    o_ref[...] = _rms(x_ref[...], g_ref[...])


def final_norm(x, g, *, tm):
    n, d = x.shape
    return pl.pallas_call(
        _final_norm_kernel,
        grid=(n // tm,),
        in_specs=[pl.BlockSpec((tm, d), lambda i: (i, 0)), pl.BlockSpec((1, d), lambda i: (0, 0))],
        out_specs=pl.BlockSpec((tm, d), lambda i: (i, 0)),
        out_shape=jax.ShapeDtypeStruct((n, d), F32),
        compiler_params=_cparams(("parallel",)),
        name="final_norm",
    )(x, g.reshape(1, d))


def _rope_tables(pos):
    half = QK_ROPE_DIM // 2
    inv_freq = ROPE_THETA ** (-jnp.arange(half, dtype=F32) / half)
    ang = pos.astype(F32)[:, None] * inv_freq[None, :]
    cos = jnp.tile(jnp.cos(ang), (1, 2))
    sin = jnp.tile(jnp.sin(ang), (1, 2))
    z = jnp.zeros_like(cos)
    cat = lambda a, b: jnp.concatenate([a, b], axis=1)
    return (cat(cos, z), cat(z, cos), cat(sin, z), cat(z, sin), cat(cos, cos), cat(sin, sin))


def _rot_cols(w):
    half = QK_ROPE_DIM // 2
    return jnp.concatenate([-w[..., half:], w[..., :half]], axis=-1)


def _router_weights(w_router):
    w = jnp.pad(w_router, ((0, 0), (0, LANES - N_EXPERTS)))
    hi = w.astype(BF16)
    lo = (w - hi.astype(F32)).astype(BF16)
    return jnp.concatenate([hi, lo], axis=1)


def _mla_weights(w_uq, w_dkv):
    ql = w_uq.shape[0]
    wq = w_uq.reshape(ql, MLA_HEADS, QK_NOPE_DIM + QK_ROPE_DIM)
    nope, pe = wq[..., :QK_NOPE_DIM], wq[..., QK_NOPE_DIM:]
    z = jnp.zeros_like(pe)
    even = jnp.concatenate([nope, pe, z], axis=-1)
    odd = jnp.concatenate([nope, z, pe], axis=-1)
    is_even = (jnp.arange(MLA_HEADS) % 2 == 0)[None, :, None]
    w_uq_ext = jnp.where(is_even, even, odd).reshape(ql, MLA_HEADS * Q_HEAD_W).astype(BF16)
    w_uq_rot = _rot_cols(pe).reshape(ql, MLA_HEADS * QK_ROPE_DIM).astype(BF16)
    wc, wp = w_dkv[:, :KV_LORA], w_dkv[:, KV_LORA:]
    wr = _rot_cols(wp)
    w_dkv_ext = jnp.concatenate([wc, wp, wp, wr, wr], axis=1).astype(BF16)
    return w_uq_ext, w_uq_rot, w_dkv_ext


def kernel(x_prompt, x_sample, cache_l1_ckv, cache_l1_kpe, cache_l3_ckv, cache_l3_kpe, cache_mem_k, cache_mem_v, mem_prompt, g_mix, g_xattn, g_mem, g_ffn, g_final, sgu_w_in, sgu_b_in, sgu_ln_g, sgu_ln_b, sgu_w_s, sgu_b_s, sgu_w_out, mla_w_dq, mla_g_q, mla_w_uq, mla_w_dkv, mla_g_kv, mla_w_uk, mla_w_uv, mla_w_o, xa_w_q, xa_w_k, xa_w_v, xa_w_o, ffn_w_gate, ffn_w_up, ffn_w_down, moe_w_router, moe_w_gate, moe_w_up, moe_w_down):
    bp, sp, d = x_prompt.shape
    bs, ss, _ = x_sample.shape
    past = cache_l1_ckv.shape[1]
    depth = g_mix.shape[0]
    n_p, n_s = bp * sp, bs * ss
    mem_tokens = mem_prompt.shape[1]
    xa_w = XA_HEADS * XA_HEAD_DIM
    mla_cache = ((cache_l1_ckv, cache_l1_kpe), (cache_l3_ckv, cache_l3_kpe))

    bf = lambda a: a.astype(BF16)
    xp = x_prompt.reshape(n_p, d)
    xs = x_sample.reshape(n_s, d)
    tm_p, tm_s = min(ROW_TILE, n_p), min(ROW_TILE, n_s)
    tq_p = min(ROW_TILE, sp)
    tf = min(FF_TILE, ffn_w_gate.shape[2])

    mk_all, mv_all = mem_kv(mem_prompt.reshape(bp * mem_tokens, d), g_mem, bf(xa_w_k), bf(xa_w_v),
                            tm=min(ROW_TILE, bp * mem_tokens))
    new_mem_k = mk_all.reshape(depth, bp, mem_tokens, XA_HEADS, XA_HEAD_DIM)
    new_mem_v = mv_all.reshape(depth, bp, mem_tokens, XA_HEADS, XA_HEAD_DIM)

    tabs_p = _rope_tables(jnp.arange(sp, dtype=jnp.int32))
    tabs_s = _rope_tables(past + jnp.arange(ss, dtype=jnp.int32))

    ckv_p, kpe_p, ckv_s, kpe_s, sgu_v_s = [], [], [], [], []
    for layer in range(depth):
        i = layer // 2
        if layer % 2 == 0:
            w_in, w_out = bf(sgu_w_in[i]), bf(sgu_w_out[i])
            sgu_args = (w_in, sgu_b_in[i], sgu_ln_g[i], sgu_ln_b[i])
            ffn_w = (ffn_w_gate, ffn_w_up, ffn_w_down)
            ride = all(_cast_row_block(w.shape[1], 1, 2 * (n_p // tm_p)) for w in ffn_w)
            zp, _, ffn_bf = sgu_in(xp, g_mix[layer], *sgu_args, [(w, i, 1) for w in ffn_w] if ride else [],
                                   tm=tm_p, emit_v=False)
            if not ride:
                ffn_bf = [cast_group(w, i, 1) for w in ffn_w]
            zs, v_new, _ = sgu_in(xs, g_mix[layer], *sgu_args, tm=tm_s, emit_v=True)
            sgu_v_s.append(v_new.reshape(bs, ss, -1))
            b_col = sgu_b_s[i][:, :, None]
            xp = sgu_mix(xp, zp, sgu_w_s[i], b_col, w_out, tm=MIX_TILE, seg=SGU_CHUNK)
            rep = SGU_CHUNK // ss
            w_s_s = jnp.tile(sgu_w_s[i][:, :ss, :ss], (1, rep, rep))
            b_s_s = jnp.tile(sgu_b_s[i][:, :ss], (1, rep))[:, :, None]
            xs = sgu_mix(xs, zs, w_s_s, b_s_s, w_out, tm=MIX_TILE, seg=ss)
        else:
            w_uq_ext, w_uq_rot, w_dkv_ext = _mla_weights(mla_w_uq[i], mla_w_dkv[i])
            proj_w = (bf(mla_w_dq[i]), mla_g_q[i], w_uq_ext, w_uq_rot, w_dkv_ext, mla_g_kv[i])
            w_uk, w_uv, w_o = bf(mla_w_uk[i]), bf(mla_w_uv[i]), bf(mla_w_o[i])
            q, ckv, ckv_b, kpe = mla_proj(xp, g_mix[layer], *proj_w, tabs_p, tm=tq_p, tab_blocks=sp // tq_p)
            ckv_p.append(ckv.reshape(bp, sp, KV_LORA))
            kpe_p.append(kpe[:, :QK_ROPE_DIM].reshape(bp, sp, QK_ROPE_DIM))
            kn, vv = kv_up(ckv_b, w_uk, w_uv, tm=tm_p)
            hm = lambda a, b, s: a.reshape(MLA_HEADS, b, s, a.shape[-1])
            moe_w = [w.reshape((-1,) + w.shape[2:]) for w in (moe_w_gate, moe_w_up, moe_w_down)]
            ride = all(_cast_row_block(w.shape[1], N_EXPERTS, _attn_steps(bp, sp, tq_p)) for w in moe_w)
            cast = [(w, i * N_EXPERTS, N_EXPERTS) for w in moe_w] if ride else []
            o, moe_bf = attention(hm(q, bp, sp), hm(kn, bp, sp), bf(kpe).reshape(bp, sp, LANES), hm(vv, bp, sp),
                                  cast, t=tq_p)
            if not ride:
                moe_bf = [cast_group(w, i * N_EXPERTS, N_EXPERTS) for w in moe_w]
            xp = proj_res(xp, o.reshape(n_p, -1), w_o, tm=tm_p)
            q, ckv, ckv_b, kpe = mla_proj(xs, g_mix[layer], *proj_w, tabs_s, tm=ss, tab_blocks=1)
            ckv_s.append(ckv.reshape(bs, ss, KV_LORA))
            kpe_s.append(kpe[:, :QK_ROPE_DIM].reshape(bs, ss, QK_ROPE_DIM))
            c_ckv, c_kpe = mla_cache[i]
            kv_len = past + ss
            sk = -(-kv_len // LANES) * LANES
            pad = sk - kv_len
            ckv_all = jnp.concatenate([bf(c_ckv), ckv_b.reshape(bs, ss, KV_LORA),
                                       jnp.zeros((bs, pad, KV_LORA), BF16)], axis=1)
            kpe_all = jnp.concatenate([bf(jnp.tile(c_kpe, (1, 1, 2))), bf(kpe).reshape(bs, ss, LANES),
                                       jnp.zeros((bs, pad, LANES), BF16)], axis=1)
            o = latent_attention(hm(q, bs, ss), ckv_all, kpe_all, w_uk, w_uv, kv_len=kv_len)
            xs = proj_res(xs, o.reshape(n_s, -1), w_o, tm=tm_s)

        w_q, w_o = bf(xa_w_q[layer]), bf(xa_w_o[layer])
        is_moe = layer % 2 == 1
        router = (g_ffn[layer], _router_weights(moe_w_router[i])) if is_moe else None
        rp = xattn(xp.reshape(bp, sp, d), g_xattn[layer], w_q, bf(mk_all[layer]).reshape(bp, mem_tokens, xa_w),
                   bf(mv_all[layer]).reshape(bp, mem_tokens, xa_w), w_o, router, tm=tq_p)
        rs = xattn(xs.reshape(bs, ss, d), g_xattn[layer], w_q,
                   bf(cache_mem_k[layer]).reshape(bs, mem_tokens, xa_w),
                   bf(cache_mem_v[layer]).reshape(bs, mem_tokens, xa_w), w_o, router, tm=ss)
        xp, xs = rp[0].reshape(n_p, d), rs[0].reshape(n_s, d)

        if not is_moe:
            wg, wu, wd = ffn_bf
            ffn = lambda x, tm: glu(x, g_ffn[layer], wg, wu, wd, jnp.zeros((x.shape[0] // tm,), jnp.int32),
                                    jnp.full((1,), x.shape[0] // tm, jnp.int32), tm=tm, tf=tf, dense=True)
            xp, xs = ffn(xp, tm_p), ffn(xs, tm_s)
        else:
            wg, wu, wd = moe_bf
            routed = [(r[1].reshape(-1, LANES), r[2].reshape(-1, LANES)) for r in (rp, rs)]
            xp, xs = moe([xp, xs], routed, g_ffn[layer], wg, wu, wd, g_final, tm_tok=MOE_ROW_TILE, tm_blk=tm_p, tf=tf,
                         final=layer == depth - 1)

    if depth % 2 == 1:
        xp, xs = final_norm(xp, g_final, tm=tm_p), final_norm(xs, g_final, tm=tm_s)
    y_prompt = xp.reshape(bp, sp, d)
    y_sample = xs.reshape(bs, ss, d)
    return (y_prompt, y_sample,
            ckv_p[0], kpe_p[0], ckv_p[1], kpe_p[1],
            new_mem_k, new_mem_v,
            ckv_s[0], kpe_s[0], ckv_s[1], kpe_s[1],
            sgu_v_s[0], sgu_v_s[1])
```
